```python
import jax, jax.numpy as jnp
from jax import lax
import numpy as np

D_MODEL = 1024
BATCH = 4
SEQ = 4096
DEPTH = 1

CONV_DIM = 512
CONV_WIDTH = 3
N_HEADS = 16
N_KV_HEADS = 4
HEAD_DIM = 64
GROUP = N_HEADS // N_KV_HEADS
CMP_LEN = 32
CMP_STRIDE = 16
CMP_HIDDEN = 256
SEL_LEN = 64
SEL_TOP = 16
WINDOW = 512
Q_BLOCK = 64
N_NSA_BRANCHES = 3
D_FF = -(-8 * D_MODEL // (3 * 256)) * 256
EPS = 1e-6

Q_COLS = N_HEADS * HEAD_DIM
KV_COLS = N_KV_HEADS * HEAD_DIM
IN_SIZES = (CONV_DIM, CONV_DIM, CONV_DIM, Q_COLS,
            KV_COLS, KV_COLS, KV_COLS, KV_COLS, KV_COLS, KV_COLS,
            N_NSA_BRANCHES * N_HEADS, D_MODEL, D_MODEL)
IN_COLS = sum(IN_SIZES)

kernel_name = "hybrid_shortconv_nsa_gated_merge"


def _split_points():
    pts, acc = [], 0
    for s in IN_SIZES[:-1]:
        acc += s
        pts.append(acc)
    return pts


def rmsnorm(x, g):
    xf = x.astype(jnp.float32)
    y = xf * lax.rsqrt(jnp.mean(xf * xf, axis=-1, keepdims=True) + EPS)
    return (y * g.astype(jnp.float32)).astype(x.dtype)


def masked_softmax(s, mask):
    s = jnp.where(mask, s.astype(jnp.float32), -jnp.inf)
    m = jnp.max(s, axis=-1, keepdims=True)
    m = jnp.where(jnp.isfinite(m), m, 0.0)
    e = jnp.exp(s - m)
    return e / jnp.maximum(jnp.sum(e, axis=-1, keepdims=True), 1e-30)


def short_conv_mixer(b_gate, c_gate, h_in, conv_w, w_out):
    S = h_in.shape[1]
    u = c_gate * h_in
    up = jnp.pad(u, ((0, 0), (CONV_WIDTH - 1, 0), (0, 0)))
    conv = sum(up[:, k:k + S, :] * conv_w[k] for k in range(CONV_WIDTH))
    return (b_gate * conv) @ w_out


def compress(kv, pos, w1, w2, n_cmp):
    B = kv.shape[0]
    idx = jnp.arange(n_cmp)[:, None] * CMP_STRIDE + jnp.arange(CMP_LEN)[None, :]
    blk = kv[:, idx] + pos[None, None, :, None, :]
    blk = blk.transpose(0, 3, 1, 2, 4).reshape(B, N_KV_HEADS, n_cmp, CMP_LEN * HEAD_DIM)
    return jax.nn.gelu(blk @ w1) @ w2


def overlap_matrix(n_cmp, n_sel):
    cs = np.arange(n_cmp)[:, None] * CMP_STRIDE
    ss = np.arange(n_sel)[None, :] * SEL_LEN
    ov = np.maximum(0, np.minimum(cs + CMP_LEN, ss + SEL_LEN) - np.maximum(cs, ss))
    return jnp.asarray(ov / CMP_LEN, dtype=jnp.float32)


def nsa_attention(q, k_cmp, v_cmp, k_sel, v_sel, k_win, v_win, overlap, n_top):
    B, _, _, S, _ = q.shape
    n_cmp = k_cmp.shape[2]
    n_sel = k_sel.shape[2]
    scale = HEAD_DIM ** -0.5
    cmp_end = jnp.arange(n_cmp) * CMP_STRIDE + CMP_LEN - 1
    blk_ids = jnp.arange(n_sel)
    bi = jnp.arange(B)[:, None, None]
    hi = jnp.arange(N_KV_HEADS)[None, :, None]

    def block(b):
        q0 = b * Q_BLOCK
        qb = lax.dynamic_slice_in_dim(q, q0, Q_BLOCK, axis=3)
        t = q0 + jnp.arange(Q_BLOCK)
        s = jnp.einsum('bhgqd,bhnd->bhgqn', qb, k_cmp) * scale
        p_cmp = masked_softmax(s, cmp_end[None, :] <= t[:, None])
        o_cmp = jnp.einsum('bhgqn,bhnd->bhgqd', p_cmp.astype(v_cmp.dtype), v_cmp)
        imp = jnp.einsum('bhgqn,nm->bhqm', p_cmp, overlap)
        cur = t // SEL_LEN
        forced = (blk_ids[None, :] == 0) | (blk_ids[None, :] == cur[:, None]) | (blk_ids[None, :] == cur[:, None] - 1)
        imp = jnp.where(forced, jnp.inf, imp)
        imp = jnp.where(blk_ids[None, :] * SEL_LEN <= t[:, None], imp, -jnp.inf)
        top_val, top_idx = lax.top_k(imp, n_top)
        blk_ok = top_val > -jnp.inf
        flat = top_idx.reshape(B, N_KV_HEADS, Q_BLOCK * n_top)
        ks = k_sel[bi, hi, flat].reshape(B, N_KV_HEADS, Q_BLOCK, n_top * SEL_LEN, HEAD_DIM)
        vs = v_sel[bi, hi, flat].reshape(B, N_KV_HEADS, Q_BLOCK, n_top * SEL_LEN, HEAD_DIM)
        kpos = top_idx[..., None] * SEL_LEN + jnp.arange(SEL_LEN)
        smask = (kpos <= t[None, None, :, None, None]) & blk_ok[..., None]
        smask = smask.reshape(B, N_KV_HEADS, Q_BLOCK, n_top * SEL_LEN)
        s = jnp.einsum('bhgqd,bhqkd->bhgqk', qb, ks) * scale
        p = masked_softmax(s, smask[:, :, None])
        o_sel = jnp.einsum('bhgqk,bhqkd->bhgqd', p.astype(vs.dtype), vs)
        kw = lax.dynamic_slice_in_dim(k_win, q0, Q_BLOCK + WINDOW, axis=2)
        vw = lax.dynamic_slice_in_dim(v_win, q0, Q_BLOCK + WINDOW, axis=2)
        wpos = q0 - WINDOW + jnp.arange(Q_BLOCK + WINDOW)
        diff = t[:, None] - wpos[None, :]
        wmask = (diff >= 0) & (diff < WINDOW) & (wpos[None, :] >= 0)
        s = jnp.einsum('bhgqd,bhkd->bhgqk', qb, kw) * scale
        p = masked_softmax(s, wmask)
        o_win = jnp.einsum('bhgqk,bhkd->bhgqd', p.astype(vw.dtype), vw)
        return jnp.stack([o_cmp, o_sel, o_win], axis=0)

    out = lax.map(block, jnp.arange(S // Q_BLOCK))
    out = out.transpose(1, 2, 0, 5, 3, 4, 6)
    return out.reshape(N_NSA_BRANCHES, B, S, N_HEADS, HEAD_DIM)


def setup_inputs(seed: int = 0) -> dict:
    key = jax.random.key(seed)
    ks = jax.random.split(key, 20)
    L = DEPTH
    nrm = lambda k, shape, fan_in: jax.random.normal(k, shape, jnp.float32) * (fan_in ** -0.5)
    gain = lambda k, shape: 1.0 + 0.05 * jax.random.normal(k, shape, jnp.float32)
    return {
        "x": jax.random.normal(ks[0], (BATCH, SEQ, D_MODEL), jnp.float32),
        "w_in": nrm(ks[1], (L, D_MODEL, IN_COLS), D_MODEL),
        "conv_w": nrm(ks[2], (L, CONV_WIDTH, CONV_DIM), CONV_WIDTH),
        "w_conv_out": nrm(ks[3], (L, CONV_DIM, D_MODEL), CONV_DIM),
        "cmp_pos_k": 0.1 * jax.random.normal(ks[4], (L, CMP_LEN, HEAD_DIM), jnp.float32),
        "cmp_w1_k": nrm(ks[5], (L, CMP_LEN * HEAD_DIM, CMP_HIDDEN), CMP_LEN * HEAD_DIM),
        "cmp_w2_k": nrm(ks[6], (L, CMP_HIDDEN, HEAD_DIM), CMP_HIDDEN),
        "cmp_pos_v": 0.1 * jax.random.normal(ks[7], (L, CMP_LEN, HEAD_DIM), jnp.float32),
        "cmp_w1_v": nrm(ks[8], (L, CMP_LEN * HEAD_DIM, CMP_HIDDEN), CMP_LEN * HEAD_DIM),
        "cmp_w2_v": nrm(ks[9], (L, CMP_HIDDEN, HEAD_DIM), CMP_HIDDEN),
        "w_attn_out": nrm(ks[10], (L, Q_COLS, D_MODEL), Q_COLS),
        "w_o": nrm(ks[11], (L, D_MODEL, D_MODEL), D_MODEL),
        "g_mix": gain(ks[12], (L, D_MODEL)),
        "g_ffn": gain(ks[13], (L, D_MODEL)),
        "w_gate": nrm(ks[14], (L, D_MODEL, D_FF), D_MODEL),
        "w_up": nrm(ks[15], (L, D_MODEL, D_FF), D_MODEL),
        "w_down": nrm(ks[16], (L, D_FF, D_MODEL), D_FF),
        "g_final": gain(ks[17], (D_MODEL,)),
    }


def reference(x, w_in, conv_w, w_conv_out, cmp_pos_k, cmp_w1_k, cmp_w2_k, cmp_pos_v, cmp_w1_v, cmp_w2_v,
              w_attn_out, w_o, g_mix, g_ffn, w_gate, w_up, w_down, g_final):
    B, S, _ = x.shape
    n_cmp = (S - CMP_LEN) // CMP_STRIDE + 1
    n_sel = S // SEL_LEN
    n_top = min(SEL_TOP, n_sel)
    overlap = overlap_matrix(n_cmp, n_sel)
    split_pts = _split_points()
    h = x
    for l in range(DEPTH):
        n = rmsnorm(h, g_mix[l])
        proj = n @ w_in[l]
        (b_gate, c_gate, h_conv, q, k_c, v_c, k_s, v_s, k_w, v_w,
         g_br, g_conv, g_attn) = jnp.split(proj, split_pts, axis=-1)
        y_conv = short_conv_mixer(b_gate, c_gate, h_conv, conv_w[l], w_conv_out[l])
        q = q.reshape(B, S, N_KV_HEADS, GROUP, HEAD_DIM).transpose(0, 2, 3, 1, 4)
        kvh = lambda t: t.reshape(B, S, N_KV_HEADS, HEAD_DIM)
        k_cmp = compress(kvh(k_c), cmp_pos_k[l], cmp_w1_k[l], cmp_w2_k[l], n_cmp)
        v_cmp = compress(kvh(v_c), cmp_pos_v[l], cmp_w1_v[l], cmp_w2_v[l], n_cmp)
        selb = lambda t: kvh(t).reshape(B, n_sel, SEL_LEN, N_KV_HEADS, HEAD_DIM).transpose(0, 3, 1, 2, 4)
        winp = lambda t: jnp.pad(kvh(t).transpose(0, 2, 1, 3), ((0, 0), (0, 0), (WINDOW, 0), (0, 0)))
        o_all = nsa_attention(q, k_cmp, v_cmp, selb(k_s), selb(v_s), winp(k_w), winp(v_w), overlap, n_top)
        br_gates = jax.nn.sigmoid(g_br).reshape(B, S, N_NSA_BRANCHES, N_HEADS)
        o = jnp.einsum('bsch,cbshd->bshd', br_gates, o_all).reshape(B, S, Q_COLS)
        y_attn = o @ w_attn_out[l]
        mix = jax.nn.sigmoid(g_conv) * y_conv + jax.nn.sigmoid(g_attn) * y_attn
        h = h + mix @ w_o[l]
        n = rmsnorm(h, g_ffn[l])
        h = h + (jax.nn.silu(n @ w_gate[l]) * (n @ w_up[l])) @ w_down[l]
    return rmsnorm(h, g_final)
```

```python
import functools

import numpy as np
import jax
import jax.numpy as jnp
from jax import lax
from jax.experimental import pallas as pl
from jax.experimental.pallas import tpu as pltpu

D_MODEL = 1024
CONV_DIM = 512
CONV_WIDTH = 3
N_HEADS = 16
N_KV_HEADS = 4
HEAD_DIM = 64
GROUP = N_HEADS // N_KV_HEADS
CMP_LEN = 32
CMP_STRIDE = 16
CMP_HIDDEN = 256
SEL_LEN = 64
SEL_TOP = 16
WINDOW = 512
N_BRANCH = 3
D_FF = 2816
EPS = 1e-6

Q_COLS = N_HEADS * HEAD_DIM
KV_COLS = N_KV_HEADS * HEAD_DIM
GQ_COLS = GROUP * HEAD_DIM
LANES = 128
VMEM_LIMIT = 56 * 1024 * 1024

ROW_TILE = 512
Q_TILE = 256
K_TILE = 256

NEG = -(2.0 ** 100)
RANK_FORCED = 8.0
RANK_INVALID = -1.0

BF16 = jnp.bfloat16
F32 = jnp.float32


def _dot(a, b):
    return jnp.dot(a, b, preferred_element_type=F32)


def _dot_nt(a, b):
    return lax.dot_general(a, b, (((1,), (1,)), ((), ())), preferred_element_type=F32)


def _rmsnorm(x, g):
    y = x * lax.rsqrt(jnp.mean(x * x, axis=-1, keepdims=True) + EPS)
    return y * g


def _params(*semantics):
    return pltpu.CompilerParams(dimension_semantics=semantics, vmem_limit_bytes=VMEM_LIMIT)


def _resident(shape):
    zeros = (0,) * len(shape)
    return pl.BlockSpec(shape, lambda *_: zeros, pipeline_mode=pl.Buffered(1))


IN_GROUPS = (
    ("bch", 3 * CONV_DIM, BF16),
    ("q", Q_COLS, BF16),
    ("kvc", 2 * KV_COLS, F32),
    ("kvs", 2 * KV_COLS, BF16),
    ("kvw", 2 * KV_COLS, BF16),
    ("gbr", N_KV_HEADS * LANES, F32),
    ("gmix", 2 * D_MODEL, BF16),
)
IN_COLS_PADDED = sum(w for _, w, _ in IN_GROUPS)


def _arrange_w_in(w_in):
    sizes = (CONV_DIM, CONV_DIM, CONV_DIM, Q_COLS) + (KV_COLS,) * 6 + (N_BRANCH * N_HEADS, D_MODEL, D_MODEL)
    pts = np.cumsum(sizes)[:-1]
    (b, c, h, q, k_c, v_c, k_s, v_s, k_w, v_w, g_br, g_conv, g_attn) = jnp.split(w_in, pts, axis=-1)

    def per_head(k, v):
        k = k.reshape(D_MODEL, N_KV_HEADS, HEAD_DIM)
        v = v.reshape(D_MODEL, N_KV_HEADS, HEAD_DIM)
        return jnp.concatenate([k, v], axis=-1).reshape(D_MODEL, 2 * KV_COLS)

    g_br = g_br.reshape(D_MODEL, N_BRANCH, N_KV_HEADS, GROUP).transpose(0, 2, 1, 3)
    g_br = g_br.reshape(D_MODEL, N_KV_HEADS, N_BRANCH * GROUP)
    g_br = jnp.pad(g_br, ((0, 0), (0, 0), (0, LANES - N_BRANCH * GROUP))).reshape(D_MODEL, N_KV_HEADS * LANES)
    cols = [b, c, h, q * (HEAD_DIM ** -0.5), k_c, v_c, per_head(k_s, v_s), per_head(k_w, v_w), g_br, g_conv, g_attn]
    return jnp.concatenate(cols, axis=-1).astype(BF16)


def _in_proj_kernel(x_ref, g_ref, w_ref, *out_refs):
    n = _rmsnorm(x_ref[...], g_ref[...]).astype(BF16)
    off = 0
    for ref, (_, width, _) in zip(out_refs, IN_GROUPS):
        for c0 in range(0, width, 512):
            cw = min(512, width - c0)
            ref[:, c0:c0 + cw] = _dot(n, w_ref[:, off + c0:off + c0 + cw]).astype(ref.dtype)
        off += width


def _in_proj(x2, g_mix, w_arr):
    t = x2.shape[0]
    row = lambda w: pl.BlockSpec((ROW_TILE, w), lambda i: (i, 0))
    return pl.pallas_call(
        _in_proj_kernel,
        grid=(t // ROW_TILE,),
        in_specs=[row(D_MODEL), _resident((1, D_MODEL)), _resident((D_MODEL, IN_COLS_PADDED))],
        out_specs=[row(w) for _, w, _ in IN_GROUPS],
        out_shape=[jax.ShapeDtypeStruct((t, w), dt) for _, w, dt in IN_GROUPS],
        compiler_params=_params("arbitrary"),
        name="in_proj",
    )(x2, g_mix, w_arr)


def _compress_kernel(x_ref, pos_ref, w1_ref, w2_ref, out_ref, *, n_chunk):
    lane = lax.broadcasted_iota(jnp.int32, (n_chunk, LANES), 1)
    half = CMP_LEN // 2
    for h in range(LANES // HEAD_DIM):
        in_head = (lane >= h * HEAD_DIM) & (lane < (h + 1) * HEAD_DIM)
        top = jnp.zeros((n_chunk, CMP_HIDDEN), F32)
        bot = jnp.zeros((n_chunk, CMP_HIDDEN), F32)
        for l in range(half):
            x = x_ref[pl.ds(l, n_chunk, stride=CMP_STRIDE), :]
            xt = jnp.where(in_head, x + pos_ref[l:l + 1, :], 0.0).astype(BF16)
            xb = jnp.where(in_head, x + pos_ref[half + l:half + l + 1, :], 0.0).astype(BF16)
            top = top + _dot(xt, w1_ref[l])
            bot = bot + _dot(xb, w1_ref[half + l])
        pre = top + pltpu.roll(bot, n_chunk - 1, 0)
        hid = jax.nn.gelu(pre).astype(BF16)
        for g in range(GROUP):
            out_ref[h, g] = _dot(hid, w2_ref[g]).astype(BF16)


def _compress(kvc, pos2, w1t, w2e, batch, seq):
    n_chunk = seq // CMP_STRIDE
    pair = LANES // HEAD_DIM
    n_pair = N_KV_HEADS // pair
    kern = functools.partial(_compress_kernel, n_chunk=n_chunk)
    return pl.pallas_call(
        kern,
        grid=(2, batch, n_pair),
        in_specs=[
            pl.BlockSpec((seq, LANES), lambda kv, b, hp: (b, kv * n_pair + hp)),
            pl.BlockSpec((None, CMP_LEN, LANES), lambda kv, b, hp: (kv, 0, 0)),
            pl.BlockSpec((None, CMP_LEN, LANES, CMP_HIDDEN), lambda kv, b, hp: (kv, 0, 0, 0)),
            pl.BlockSpec((None, GROUP, CMP_HIDDEN, GQ_COLS), lambda kv, b, hp: (kv, 0, 0, 0)),
        ],
        out_specs=pl.BlockSpec((None, None, pair, GROUP, n_chunk, GQ_COLS),
                               lambda kv, b, hp: (kv, b, hp, 0, 0, 0)),
        out_shape=jax.ShapeDtypeStruct((2, batch, N_KV_HEADS, GROUP, n_chunk, GQ_COLS), BF16),
        compiler_params=_params("arbitrary", "arbitrary", "arbitrary"),
        name="compress",
    )(kvc, pos2, w1t, w2e)


def _cmp_attn_kernel(q_ref, kc_ref, vc_ref, ovt_ref, ocmp_ref, bias_ref, rank_ref, *, n_chunk, n_sel, n_top):
    i = pl.program_id(2)
    q0 = i * Q_TILE
    q = q_ref[...]
    t_col = q0 + lax.broadcasted_iota(jnp.int32, (Q_TILE, 1), 0)
    n_row = lax.broadcasted_iota(jnp.int32, (1, n_chunk), 1)
    cmp_ok = (n_row * CMP_STRIDE + (CMP_LEN - 1) <= t_col) & (n_row < n_chunk - 1)

    p_sum = jnp.zeros((Q_TILE, n_chunk), F32)
    o = jnp.zeros((Q_TILE, GQ_COLS), F32)
    for g in range(GROUP):
        s = jnp.where(cmp_ok, _dot_nt(q, kc_ref[g]), -jnp.inf)
        m = jnp.max(s, axis=-1, keepdims=True)
        m = jnp.where(jnp.isfinite(m), m, 0.0)
        e = jnp.exp(s - m)
        p = e / jnp.maximum(jnp.sum(e, axis=-1, keepdims=True), 1e-30)
        o = o + _dot(p.astype(BF16), vc_ref[g])
        p_sum = p_sum + p
    ocmp_ref[...] = o

    ovt = ovt_ref[...]
    p1 = p_sum.astype(BF16)
    r1 = p_sum - p1.astype(F32)
    p2 = r1.astype(BF16)
    p3 = (r1 - p2.astype(F32)).astype(BF16)
    imp = _dot_nt(ovt, p1) + _dot_nt(ovt, p2) + _dot_nt(ovt, p3)

    n_blk = ovt_ref.shape[0]
    j_col = lax.broadcasted_iota(jnp.int32, (n_blk, 1), 0)
    t_row = q0 + lax.broadcasted_iota(jnp.int32, (1, Q_TILE), 1)
    cur = t_row // SEL_LEN
    forced = (j_col == 0) | (j_col == cur) | (j_col == cur - 1)
    valid = (j_col * SEL_LEN <= t_row) & (j_col < n_sel)
    v = jnp.where(valid, jnp.where(forced, RANK_FORCED, imp), RANK_INVALID)
    rank_ref[...] = v
    rank = jnp.zeros((n_blk, Q_TILE), jnp.int32)
    for jp in range(n_sel):
        row = rank_ref[jp:jp + 1, :]
        rank = rank + jnp.where(j_col > jp, jnp.where(row >= v, 1, 0), jnp.where(row > v, 1, 0))
    sel = (rank < n_top) & valid
    bias_t = jnp.where(sel, 0.0, NEG)
    if n_blk < LANES:
        bias_t = jnp.concatenate([bias_t, jnp.full((LANES - n_blk, Q_TILE), NEG, F32)], axis=0)
    bias_ref[...] = bias_t.T[:, :SEL_LEN].astype(BF16)


def _cmp_attn(q, cmp_e, ovt, batch, seq, n_sel, n_top):
    nq = seq // Q_TILE
    n_chunk = seq // CMP_STRIDE
    n_blk = ovt.shape[0]
    kern = functools.partial(_cmp_attn_kernel, n_chunk=n_chunk, n_sel=n_sel, n_top=n_top)
    cmp_spec = lambda kv: pl.BlockSpec((None, None, None, GROUP, n_chunk, GQ_COLS),
                                       lambda b, h, i: (kv, b, h, 0, 0, 0))
    return pl.pallas_call(
        kern,
        grid=(batch, N_KV_HEADS, nq),
        in_specs=[
            pl.BlockSpec((Q_TILE, GQ_COLS), lambda b, h, i: (b * nq + i, h)),
            cmp_spec(0), cmp_spec(1),
            pl.BlockSpec((n_blk, n_chunk), lambda b, h, i: (0, 0)),
        ],
        out_specs=[
            pl.BlockSpec((Q_TILE, GQ_COLS), lambda b, h, i: (b * nq + i, h)),
            pl.BlockSpec((None, None, Q_TILE, SEL_LEN), lambda b, h, i: (b, h, i, 0)),
        ],
        out_shape=[
            jax.ShapeDtypeStruct((batch * seq, Q_COLS), F32),
            jax.ShapeDtypeStruct((batch, N_KV_HEADS, seq, SEL_LEN), BF16),
        ],
        scratch_shapes=[pltpu.VMEM((n_blk, Q_TILE), F32)],
        compiler_params=_params("arbitrary", "arbitrary", "arbitrary"),
        name="cmp_attn",
    )(q, cmp_e, cmp_e, ovt)


def _nsa_attn_kernel(q_ref, bias_ref, kvs_ref, kvw_ref, ocmp_ref, gbr_ref, o_ref,
                     ka_ref, va_ref, kw_ref, vw_ref, qa_ref, m_ref, acc_ref, *, seq):
    i = pl.program_id(2)
    q0 = i * Q_TILE
    rows = GROUP * Q_TILE

    @pl.when(i == 0)
    def _stage_kv():
        blk_of_key = lax.broadcasted_iota(jnp.int32, (seq, HEAD_DIM), 0) // SEL_LEN
        lane = lax.broadcasted_iota(jnp.int32, (seq, HEAD_DIM), 1)
        onehot = jnp.where(blk_of_key == lane, 1.0, 0.0).astype(BF16)
        ones = jnp.ones((seq, HEAD_DIM), BF16)
        zeros = jnp.zeros((seq, HEAD_DIM), BF16)
        kvs = kvs_ref[...]
        kvw = kvw_ref[...]
        ka_ref[...] = jnp.concatenate([kvs[:, :HEAD_DIM], onehot], axis=1)
        va_ref[...] = jnp.concatenate([kvs[:, HEAD_DIM:], ones], axis=1)
        kw_ref[...] = jnp.concatenate([kvw[:, :HEAD_DIM], zeros], axis=1)
        vw_ref[...] = jnp.concatenate([kvw[:, HEAD_DIM:], ones], axis=1)

    q = q_ref[...]
    bias = bias_ref[...]
    for g in range(GROUP):
        qa_ref[g * Q_TILE:(g + 1) * Q_TILE, :] = jnp.concatenate(
            [q[:, g * HEAD_DIM:(g + 1) * HEAD_DIM], bias], axis=1)

    r_tok = lax.broadcasted_iota(jnp.int32, (rows, 1), 0) % Q_TILE
    c_key = lax.broadcasted_iota(jnp.int32, (1, K_TILE), 1)

    def reset():
        m_ref[...] = jnp.full((rows, 1), -jnp.inf, F32)
        acc_ref[...] = jnp.zeros((rows, LANES), F32)

    def step(k_ref, v_ref, kt, mask):
        k0 = pl.multiple_of(kt * K_TILE, K_TILE)
        s = _dot_nt(qa_ref[...], k_ref[pl.ds(k0, K_TILE), :])
        if mask == "causal":
            s = jnp.where(c_key <= r_tok, s, NEG)
        elif mask == "tail":
            s = jnp.where(c_key > r_tok, s, NEG)
        m_old = m_ref[...]
        m_new = jnp.maximum(m_old, jnp.max(s, axis=-1, keepdims=True))
        p = jnp.exp(s - m_new).astype(BF16)
        acc_ref[...] = jnp.exp(m_old - m_new) * acc_ref[...] + _dot(p, v_ref[pl.ds(k0, K_TILE), :])
        m_ref[...] = m_new

    def finish():
        acc = acc_ref[...]
        return acc[:, :HEAD_DIM] / jnp.maximum(acc[:, HEAD_DIM:HEAD_DIM + 1], 1e-30)

    reset()
    lax.fori_loop(0, i, lambda kt, c: (step(ka_ref, va_ref, kt, None), c)[1], 0)
    step(ka_ref, va_ref, i, "causal")
    o_sel = finish()

    reset()
    step(kw_ref, vw_ref, i, "causal")

    @pl.when(i >= 1)
    def _():
        step(kw_ref, vw_ref, i - 1, None)

    @pl.when(i >= 2)
    def _():
        step(kw_ref, vw_ref, i - 2, "tail")

    o_win = finish()

    gate = jax.nn.sigmoid(gbr_ref[...])
    ocmp = ocmp_ref[...]
    for g in range(GROUP):
        rs = slice(g * Q_TILE, (g + 1) * Q_TILE)
        cs = slice(g * HEAD_DIM, (g + 1) * HEAD_DIM)
        o = (gate[:, g:g + 1] * ocmp[:, cs]
             + gate[:, GROUP + g:GROUP + g + 1] * o_sel[rs]
             + gate[:, 2 * GROUP + g:2 * GROUP + g + 1] * o_win[rs])
        o_ref[:, cs] = o.astype(o_ref.dtype)


def _nsa_attn(q, bias, kvs, kvw, ocmp, gbr, batch, seq):
    nq = seq // Q_TILE
    rows = GROUP * Q_TILE
    kern = functools.partial(_nsa_attn_kernel, seq=seq)
    tile = lambda w: pl.BlockSpec((Q_TILE, w), lambda b, h, i: (b * nq + i, h))
    slab = pl.BlockSpec((seq, 2 * HEAD_DIM), lambda b, h, i: (b, h))
    return pl.pallas_call(
        kern,
        grid=(batch, N_KV_HEADS, nq),
        in_specs=[
            tile(GQ_COLS),
            pl.BlockSpec((None, None, Q_TILE, SEL_LEN), lambda b, h, i: (b, h, i, 0)),
            slab, slab,
            tile(GQ_COLS),
            tile(LANES),
        ],
        out_specs=tile(GQ_COLS),
        out_shape=jax.ShapeDtypeStruct((batch * seq, Q_COLS), BF16),
        scratch_shapes=[pltpu.VMEM((seq, LANES), BF16)] * 4 + [
            pltpu.VMEM((rows, LANES), BF16),
            pltpu.VMEM((rows, 1), F32),
            pltpu.VMEM((rows, LANES), F32),
        ],
        compiler_params=_params("arbitrary", "arbitrary", "arbitrary"),
        name="nsa_attn",
    )(q, bias, kvs, kvw, ocmp, gbr)


def _merge_kernel(o_ref, bch_ref, halo_ref, gmix_ref, x_ref, cw_ref, wa_ref, wc_ref, wo_ref, h_ref, *, seq):
    i = pl.program_id(0)
    bch = bch_ref[...].astype(F32)
    b_gate, u = bch[:, :CONV_DIM], bch[:, CONV_DIM:2 * CONV_DIM] * bch[:, 2 * CONV_DIM:]
    halo = halo_ref[...].astype(F32)
    u_halo = halo[:, CONV_DIM:2 * CONV_DIM] * halo[:, 2 * CONV_DIM:]
    u_halo = jnp.where((i * ROW_TILE) % seq == 0, 0.0, u_halo)
    row = lax.broadcasted_iota(jnp.int32, (ROW_TILE, 1), 0)
    u1 = jnp.where(row == 0, u_halo[7:8], pltpu.roll(u, 1, 0))
    u2 = jnp.where(row == 0, u_halo[6:7], jnp.where(row == 1, u_halo[7:8], pltpu.roll(u, 2, 0)))
    cw = cw_ref[...]
    conv = u2 * cw[0:1] + u1 * cw[1:2] + u * cw[2:3]
    y_conv = _dot((b_gate * conv).astype(BF16), wc_ref[...])
    y_attn = _dot(o_ref[...], wa_ref[...])
    gmix = gmix_ref[...].astype(F32)
    mix = jax.nn.sigmoid(gmix[:, :D_MODEL]) * y_conv + jax.nn.sigmoid(gmix[:, D_MODEL:]) * y_attn
    h_ref[...] = x_ref[...] + _dot(mix.astype(BF16), wo_ref[...])


def _merge(o, bch, gmix, x2, conv_w, w_attn_out, w_conv_out, w_o, seq):
    t = x2.shape[0]
    row = lambda w: pl.BlockSpec((ROW_TILE, w), lambda i: (i, 0))
    halo_blocks = ROW_TILE // 8
    kern = functools.partial(_merge_kernel, seq=seq)
    return pl.pallas_call(
        kern,
        grid=(t // ROW_TILE,),
        in_specs=[
            row(Q_COLS), row(3 * CONV_DIM),
            pl.BlockSpec((8, 3 * CONV_DIM), lambda i: (jnp.maximum(i * halo_blocks - 1, 0), 0)),
            row(2 * D_MODEL), row(D_MODEL),
            _resident((CONV_WIDTH, CONV_DIM)),
            _resident((Q_COLS, D_MODEL)), _resident((CONV_DIM, D_MODEL)), _resident((D_MODEL, D_MODEL)),
        ],
        out_specs=row(D_MODEL),
        out_shape=jax.ShapeDtypeStruct((t, D_MODEL), F32),
        compiler_params=_params("arbitrary"),
        name="merge",
    )(o, bch, bch, gmix, x2, conv_w, w_attn_out, w_conv_out, w_o)


FF_CHUNK = 256


def _ffn_kernel(h_ref, gf_ref, gl_ref, wg_ref, wu_ref, wd_ref, out_ref, act_ref):
    h = h_ref[...]
    n = _rmsnorm(h, gf_ref[...]).astype(BF16)
    for c0 in range(0, D_FF, FF_CHUNK):
        cs = slice(c0, c0 + FF_CHUNK)
        act_ref[:, cs] = (jax.nn.silu(_dot(n, wg_ref[:, cs])) * _dot(n, wu_ref[:, cs])).astype(BF16)
    h = h + _dot(act_ref[...], wd_ref[...])
    out_ref[...] = _rmsnorm(h, gl_ref[...])


def _ffn(h1, g_ffn, g_final, w_gate, w_up, w_down):
    t = h1.shape[0]
    row = pl.BlockSpec((ROW_TILE, D_MODEL), lambda i: (i, 0))
    return pl.pallas_call(
        _ffn_kernel,
        grid=(t // ROW_TILE,),
        in_specs=[row, _resident((1, D_MODEL)), _resident((1, D_MODEL)),
                  _resident((D_MODEL, D_FF)), _resident((D_MODEL, D_FF)), _resident((D_FF, D_MODEL))],
        out_specs=row,
        out_shape=jax.ShapeDtypeStruct((t, D_MODEL), F32),
        scratch_shapes=[pltpu.VMEM((ROW_TILE, D_FF), BF16)],
        compiler_params=_params("arbitrary"),
        name="ffn",
    )(h1, g_ffn, g_final, w_gate, w_up, w_down)


def _overlap_t(n_chunk, n_sel, n_blk):
    cs = np.arange(n_chunk)[None, :] * CMP_STRIDE
    ss = np.arange(n_blk)[:, None] * SEL_LEN
    ov = np.maximum(0, np.minimum(cs + CMP_LEN, ss + SEL_LEN) - np.maximum(cs, ss)) / CMP_LEN
    ov[n_sel:, :] = 0
    ov[:, n_chunk - 1] = 0
    return jnp.asarray(ov, dtype=BF16)


def _place_heads(w2):
    return jnp.stack([jnp.pad(w2, ((0, 0), (g * HEAD_DIM, (GROUP - 1 - g) * HEAD_DIM))) for g in range(GROUP)])


def _tile_heads(w1):
    pair = LANES // HEAD_DIM
    w1 = w1.reshape(CMP_LEN, 1, HEAD_DIM, CMP_HIDDEN)
    return jnp.broadcast_to(w1, (CMP_LEN, pair, HEAD_DIM, CMP_HIDDEN)).reshape(CMP_LEN, LANES, CMP_HIDDEN)


def kernel(x, w_in, conv_w, w_conv_out, cmp_pos_k, cmp_w1_k, cmp_w2_k, cmp_pos_v, cmp_w1_v, cmp_w2_v,
           w_attn_out, w_o, g_mix, g_ffn, w_gate, w_up, w_down, g_final):
    batch, seq, _ = x.shape
    assert w_in.shape[0] == 1, "one layer"
    assert seq % Q_TILE == 0 and (batch * seq) % ROW_TILE == 0 and seq % ROW_TILE == 0
    n_sel = seq // SEL_LEN
    assert n_sel <= SEL_LEN, "the block one-hot occupies SEL_LEN lanes"
    n_top = min(SEL_TOP, n_sel)
    n_chunk = seq // CMP_STRIDE
    n_blk = -(-n_sel // 16) * 16

    x2 = x.reshape(batch * seq, D_MODEL)
    bch, q, kvc, kvs, kvw, gbr, gmix = _in_proj(x2, g_mix[0][None], _arrange_w_in(w_in[0]))

    pair = LANES // HEAD_DIM
    pos2 = jnp.stack([jnp.tile(cmp_pos_k[0], (1, pair)), jnp.tile(cmp_pos_v[0], (1, pair))])
    w1t = jnp.stack([_tile_heads(cmp_w1_k[0]), _tile_heads(cmp_w1_v[0])]).astype(BF16)
    w2e = jnp.stack([_place_heads(cmp_w2_k[0]), _place_heads(cmp_w2_v[0])]).astype(BF16)
    cmp_e = _compress(kvc, pos2, w1t, w2e, batch, seq)

    ocmp, bias = _cmp_attn(q, cmp_e, _overlap_t(n_chunk, n_sel, n_blk), batch, seq, n_sel, n_top)
    o = _nsa_attn(q, bias, kvs, kvw, ocmp, gbr, batch, seq)

    h1 = _merge(o, bch, gmix, x2, conv_w[0], w_attn_out[0].astype(BF16), w_conv_out[0].astype(BF16),
                w_o[0].astype(BF16), seq)
    out = _ffn(h1, g_ffn[0][None], g_final[None], w_gate[0].astype(BF16), w_up[0].astype(BF16),
               w_down[0].astype(BF16))
    return out.reshape(batch, seq, D_MODEL)
```

```python
import functools

import numpy as np
import jax
import jax.numpy as jnp
from jax import lax
from jax.experimental import pallas as pl
from jax.experimental.pallas import tpu as pltpu

D_MODEL = 1024
CONV_DIM = 512
CONV_WIDTH = 3
N_HEADS = 16
N_KV_HEADS = 4
HEAD_DIM = 64
GROUP = N_HEADS // N_KV_HEADS
CMP_LEN = 32
CMP_STRIDE = 16
CMP_HIDDEN = 256
SEL_LEN = 64
SEL_TOP = 16
WINDOW = 512
N_BRANCH = 3
D_FF = 2816
EPS = 1e-6

Q_COLS = N_HEADS * HEAD_DIM
KV_COLS = N_KV_HEADS * HEAD_DIM
GQ_COLS = GROUP * HEAD_DIM
GATE_ROWS = 16
LANES = 128
VMEM_LIMIT = 56 * 1024 * 1024

ROW_TILE = 512
Q_TILE = 256
K_TILE = 256

NEG = -(2.0 ** 100)
RANK_FORCED = 8.0
RANK_INVALID = -1.0

BF16 = jnp.bfloat16
F32 = jnp.float32


def _dot(a, b):
    return jnp.dot(a, b, preferred_element_type=F32)


def _dot_nt(a, b):
    return lax.dot_general(a, b, (((1,), (1,)), ((), ())), preferred_element_type=F32)


def _rmsnorm(x, g):
    y = x * lax.rsqrt(jnp.mean(x * x, axis=-1, keepdims=True) + EPS)
    return y * g


def _params(*semantics):
    return pltpu.CompilerParams(dimension_semantics=semantics, vmem_limit_bytes=VMEM_LIMIT)


def _resident(shape):
    zeros = (0,) * len(shape)
    return pl.BlockSpec(shape, lambda *_: zeros, pipeline_mode=pl.Buffered(1))


ROW_GROUPS = (
    ("bch", 3 * CONV_DIM, BF16),
    ("kvc", 2 * KV_COLS, F32),
    ("ksw", 2 * KV_COLS, BF16),
    ("gmix", 2 * D_MODEL, BF16),
)
COL_GROUPS = (
    ("qt", Q_COLS, BF16),
    ("vswt", 2 * KV_COLS, BF16),
    ("gbrt", N_KV_HEADS * GATE_ROWS, F32),
)
ROW_WIDTH = sum(w for _, w, _ in ROW_GROUPS)
COL_HEIGHT = sum(w for _, w, _ in COL_GROUPS)


def _arrange_w_in(w_in):
    sizes = (CONV_DIM, CONV_DIM, CONV_DIM, Q_COLS) + (KV_COLS,) * 6 + (N_BRANCH * N_HEADS, D_MODEL, D_MODEL)
    pts = np.cumsum(sizes)[:-1]
    (b, c, h, q, k_c, v_c, k_s, v_s, k_w, v_w, g_br, g_conv, g_attn) = jnp.split(w_in, pts, axis=-1)

    def per_head(a, b_):
        a = a.reshape(D_MODEL, N_KV_HEADS, HEAD_DIM)
        b_ = b_.reshape(D_MODEL, N_KV_HEADS, HEAD_DIM)
        return jnp.concatenate([a, b_], axis=-1).reshape(D_MODEL, 2 * KV_COLS)

    g_br = g_br.reshape(D_MODEL, N_BRANCH, N_KV_HEADS, GROUP).transpose(0, 2, 1, 3)
    g_br = g_br.reshape(D_MODEL, N_KV_HEADS, N_BRANCH * GROUP)
    g_br = jnp.pad(g_br, ((0, 0), (0, 0), (0, GATE_ROWS - N_BRANCH * GROUP))).reshape(D_MODEL, -1)
    w_row = jnp.concatenate([b, c, h, k_c, v_c, per_head(k_s, k_w), g_conv, g_attn], axis=-1)
    w_col = jnp.concatenate([q * (HEAD_DIM ** -0.5), per_head(v_s, v_w), g_br], axis=-1)
    return w_row.astype(BF16), w_col.T.astype(BF16)


def _in_proj_kernel(x_ref, g_ref, wr_ref, wc_ref, *out_refs):
    n = _rmsnorm(x_ref[...], g_ref[...]).astype(BF16)
    row_refs, col_refs = out_refs[:len(ROW_GROUPS)], out_refs[len(ROW_GROUPS):]
    off = 0
    for ref, (_, width, _) in zip(row_refs, ROW_GROUPS):
        for c0 in range(0, width, 512):
            cw = min(512, width - c0)
            ref[:, c0:c0 + cw] = _dot(n, wr_ref[:, off + c0:off + c0 + cw]).astype(ref.dtype)
        off += width
    off = 0
    for ref, (_, height, _) in zip(col_refs, COL_GROUPS):
        for r0 in range(0, height, 512):
            rh = min(512, height - r0)
            ref[r0:r0 + rh, :] = _dot_nt(wc_ref[off + r0:off + r0 + rh, :], n).astype(ref.dtype)
        off += height


def _in_proj(x2, g_mix, w_row, w_col):
    t = x2.shape[0]
    row = lambda w: pl.BlockSpec((ROW_TILE, w), lambda i: (i, 0))
    col = lambda h: pl.BlockSpec((h, ROW_TILE), lambda i: (0, i))
    return pl.pallas_call(
        _in_proj_kernel,
        grid=(t // ROW_TILE,),
        in_specs=[row(D_MODEL), _resident((1, D_MODEL)), _resident((D_MODEL, ROW_WIDTH)),
                  _resident((COL_HEIGHT, D_MODEL))],
        out_specs=[row(w) for _, w, _ in ROW_GROUPS] + [col(h) for _, h, _ in COL_GROUPS],
        out_shape=[jax.ShapeDtypeStruct((t, w), dt) for _, w, dt in ROW_GROUPS]
        + [jax.ShapeDtypeStruct((h, t), dt) for _, h, dt in COL_GROUPS],
        compiler_params=_params("arbitrary"),
        name="in_proj",
    )(x2, g_mix, w_row, w_col)


def _compress_kernel(xk_ref, xv_ref, pos_ref, w1_ref, w2k_ref, w2vt_ref, kc_ref, vct_ref, *, n_chunk):
    lane = lax.broadcasted_iota(jnp.int32, (n_chunk, LANES), 1)
    half = CMP_LEN // 2
    for kv, x_ref in enumerate((xk_ref, xv_ref)):
        for h in range(LANES // HEAD_DIM):
            in_head = (lane >= h * HEAD_DIM) & (lane < (h + 1) * HEAD_DIM)
            top = jnp.zeros((n_chunk, CMP_HIDDEN), F32)
            bot = jnp.zeros((n_chunk, CMP_HIDDEN), F32)
            for l in range(half):
                x = x_ref[pl.ds(l, n_chunk, stride=CMP_STRIDE), :]
                xt = jnp.where(in_head, x + pos_ref[kv, l:l + 1, :], 0.0).astype(BF16)
                xb = jnp.where(in_head, x + pos_ref[kv, half + l:half + l + 1, :], 0.0).astype(BF16)
                top = top + _dot(xt, w1_ref[kv, l])
                bot = bot + _dot(xb, w1_ref[kv, half + l])
            pre = top + pltpu.roll(bot, n_chunk - 1, 0)
            hid = jax.nn.gelu(pre).astype(BF16)
            if kv == 0:
                kc_ref[h] = _dot(hid, w2k_ref[...]).astype(BF16)
            else:
                vct_ref[h] = _dot_nt(w2vt_ref[...], hid).astype(BF16)


def _compress(kvc, pos2, w1t, w2k, w2vt, batch, seq):
    n_chunk = seq // CMP_STRIDE
    pair = LANES // HEAD_DIM
    n_pair = N_KV_HEADS // pair
    kern = functools.partial(_compress_kernel, n_chunk=n_chunk)
    return pl.pallas_call(
        kern,
        grid=(batch, n_pair),
        in_specs=[
            pl.BlockSpec((seq, LANES), lambda b, hp: (b, hp)),
            pl.BlockSpec((seq, LANES), lambda b, hp: (b, n_pair + hp)),
            _resident((2, CMP_LEN, LANES)),
            _resident((2, CMP_LEN, LANES, CMP_HIDDEN)),
            _resident((CMP_HIDDEN, HEAD_DIM)),
            _resident((HEAD_DIM, CMP_HIDDEN)),
        ],
        out_specs=[
            pl.BlockSpec((None, pair, n_chunk, HEAD_DIM), lambda b, hp: (b, hp, 0, 0)),
            pl.BlockSpec((None, pair, HEAD_DIM, n_chunk), lambda b, hp: (b, hp, 0, 0)),
        ],
        out_shape=[
            jax.ShapeDtypeStruct((batch, N_KV_HEADS, n_chunk, HEAD_DIM), BF16),
            jax.ShapeDtypeStruct((batch, N_KV_HEADS, HEAD_DIM, n_chunk), BF16),
        ],
        compiler_params=_params("arbitrary", "arbitrary"),
        name="compress",
    )(kvc, kvc, pos2, w1t, w2k, w2vt)


def _nsa_attn_kernel(qt_ref, kc_ref, vct_ref, ovt_ref, ksw_ref, vswt_ref, gbrt_ref, o_ref,
                     ka_ref, kw_ref, vat_ref, vwt_ref, qa_ref, m_ref, acc_ref, rank_ref,
                     *, seq, n_sel, n_top):
    i = pl.program_id(2)
    q0 = i * Q_TILE
    lanes = GROUP * Q_TILE
    n_chunk = seq // CMP_STRIDE
    n_blk = ovt_ref.shape[0]

    @pl.when(i == 0)
    def _stage_kv():
        blk_of_key = lax.broadcasted_iota(jnp.int32, (seq, HEAD_DIM), 0) // SEL_LEN
        lane = lax.broadcasted_iota(jnp.int32, (seq, HEAD_DIM), 1)
        onehot = jnp.where(blk_of_key == lane, 1.0, 0.0).astype(BF16)
        ksw = ksw_ref[...]
        ka_ref[...] = jnp.concatenate([ksw[:, :HEAD_DIM], onehot], axis=1)
        kw_ref[...] = jnp.concatenate([ksw[:, HEAD_DIM:], jnp.zeros((seq, HEAD_DIM), BF16)], axis=1)
        ones = jnp.ones((HEAD_DIM, K_TILE), BF16)
        for kt in range(seq // K_TILE):
            cs = slice(kt * K_TILE, (kt + 1) * K_TILE)
            vat_ref[kt] = jnp.concatenate([vswt_ref[:HEAD_DIM, cs], ones], axis=0)
            vwt_ref[kt] = jnp.concatenate([vswt_ref[HEAD_DIM:, cs], ones], axis=0)

    t_row = q0 + lax.broadcasted_iota(jnp.int32, (1, Q_TILE), 1)
    n_col = lax.broadcasted_iota(jnp.int32, (n_chunk, 1), 0)
    cmp_ok = (n_col * CMP_STRIDE + (CMP_LEN - 1) <= t_row) & (n_col < n_chunk - 1)
    kc = kc_ref[...]
    vct = vct_ref[...]
    p_sum = jnp.zeros((n_chunk, Q_TILE), F32)
    o_cmp = []
    for g in range(GROUP):
        s = jnp.where(cmp_ok, _dot(kc, qt_ref[g * HEAD_DIM:(g + 1) * HEAD_DIM, :]), -jnp.inf)
        m = jnp.max(s, axis=0, keepdims=True)
        m = jnp.where(jnp.isfinite(m), m, 0.0)
        e = jnp.exp(s - m)
        p = e / jnp.maximum(jnp.sum(e, axis=0, keepdims=True), 1e-30)
        o_cmp.append(_dot(vct, p.astype(BF16)))
        p_sum = p_sum + p

    ovt = ovt_ref[...]
    p1 = p_sum.astype(BF16)
    r1 = p_sum - p1.astype(F32)
    p2 = r1.astype(BF16)
    p3 = (r1 - p2.astype(F32)).astype(BF16)
    imp = _dot(ovt, p1) + _dot(ovt, p2) + _dot(ovt, p3)
    j_col = lax.broadcasted_iota(jnp.int32, (n_blk, 1), 0)
    cur = t_row // SEL_LEN
    forced = (j_col == 0) | (j_col == cur) | (j_col == cur - 1)
    valid = (j_col * SEL_LEN <= t_row) & (j_col < n_sel)
    v = jnp.where(valid, jnp.where(forced, RANK_FORCED, imp), RANK_INVALID)
    rank_ref[...] = v
    rank = jnp.zeros((n_blk, Q_TILE), jnp.int32)
    for jp in range(n_sel):
        row = rank_ref[jp:jp + 1, :]
        rank = rank + jnp.where(j_col > jp, jnp.where(row >= v, 1, 0), jnp.where(row > v, 1, 0))
    sel = (rank < n_top) & valid
    bias = jnp.where(sel, 0.0, NEG).astype(BF16)
    if n_blk < SEL_LEN:
        bias = jnp.concatenate([bias, jnp.full((SEL_LEN - n_blk, Q_TILE), NEG, BF16)], axis=0)
    for g in range(GROUP):
        cs = slice(g * Q_TILE, (g + 1) * Q_TILE)
        qa_ref[:HEAD_DIM, cs] = qt_ref[g * HEAD_DIM:(g + 1) * HEAD_DIM, :]
        qa_ref[HEAD_DIM:, cs] = bias

    r_key = lax.broadcasted_iota(jnp.int32, (K_TILE, 1), 0)
    c_tok = lax.broadcasted_iota(jnp.int32, (1, lanes), 1) % Q_TILE

    def reset():
        m_ref[...] = jnp.full((1, lanes), -jnp.inf, F32)
        acc_ref[...] = jnp.zeros((LANES, lanes), F32)

    def step(k_ref, vt_ref, kt, mask):
        k0 = pl.multiple_of(kt * K_TILE, K_TILE)
        s = _dot(k_ref[pl.ds(k0, K_TILE), :], qa_ref[...])
        if mask == "causal":
            s = jnp.where(r_key <= c_tok, s, NEG)
        elif mask == "tail":
            s = jnp.where(r_key > c_tok, s, NEG)
        m_old = m_ref[...]
        m_new = jnp.maximum(m_old, jnp.max(s, axis=0, keepdims=True))
        p = jnp.exp(s - m_new).astype(BF16)
        acc_ref[...] = jnp.exp(m_old - m_new) * acc_ref[...] + _dot(vt_ref[kt], p)
        m_ref[...] = m_new

    def finish():
        acc = acc_ref[...]
        return acc[:HEAD_DIM] / jnp.maximum(acc[HEAD_DIM:HEAD_DIM + 1], 1e-30)

    reset()
    lax.fori_loop(0, i, lambda kt, c: (step(ka_ref, vat_ref, kt, None), c)[1], 0)
    step(ka_ref, vat_ref, i, "causal")
    o_sel = finish()

    reset()
    step(kw_ref, vwt_ref, i, "causal")

    @pl.when(i >= 1)
    def _():
        step(kw_ref, vwt_ref, i - 1, None)

    @pl.when(i >= 2)
    def _():
        step(kw_ref, vwt_ref, i - 2, "tail")

    o_win = finish()

    gate = jax.nn.sigmoid(gbrt_ref[...])
    heads = []
    for g in range(GROUP):
        cs = slice(g * Q_TILE, (g + 1) * Q_TILE)
        heads.append(gate[g:g + 1] * o_cmp[g]
                     + gate[GROUP + g:GROUP + g + 1] * o_sel[:, cs]
                     + gate[2 * GROUP + g:2 * GROUP + g + 1] * o_win[:, cs])
    o_ref[...] = jnp.concatenate(heads, axis=0).T.astype(o_ref.dtype)


def _nsa_attn(qt, kc, vct, ovt, ksw, vswt, gbrt, batch, seq, n_sel, n_top):
    nq = seq // Q_TILE
    nkt = seq // K_TILE
    lanes = GROUP * Q_TILE
    n_chunk = seq // CMP_STRIDE
    n_blk = ovt.shape[0]
    kern = functools.partial(_nsa_attn_kernel, seq=seq, n_sel=n_sel, n_top=n_top)
    return pl.pallas_call(
        kern,
        grid=(batch, N_KV_HEADS, nq),
        in_specs=[
            pl.BlockSpec((GQ_COLS, Q_TILE), lambda b, h, i: (h, b * nq + i)),
            pl.BlockSpec((None, None, n_chunk, HEAD_DIM), lambda b, h, i: (b, h, 0, 0)),
            pl.BlockSpec((None, None, HEAD_DIM, n_chunk), lambda b, h, i: (b, h, 0, 0)),
            pl.BlockSpec((n_blk, n_chunk), lambda b, h, i: (0, 0)),
            pl.BlockSpec((seq, 2 * HEAD_DIM), lambda b, h, i: (b, h)),
            pl.BlockSpec((2 * HEAD_DIM, seq), lambda b, h, i: (h, b)),
            pl.BlockSpec((GATE_ROWS, Q_TILE), lambda b, h, i: (h, b * nq + i)),
        ],
        out_specs=pl.BlockSpec((Q_TILE, GQ_COLS), lambda b, h, i: (b * nq + i, h)),
        out_shape=jax.ShapeDtypeStruct((batch * seq, Q_COLS), BF16),
        scratch_shapes=[
            pltpu.VMEM((seq, LANES), BF16),
            pltpu.VMEM((seq, LANES), BF16),
            pltpu.VMEM((nkt, LANES, K_TILE), BF16),
            pltpu.VMEM((nkt, LANES, K_TILE), BF16),
            pltpu.VMEM((LANES, lanes), BF16),
            pltpu.VMEM((1, lanes), F32),
            pltpu.VMEM((LANES, lanes), F32),
            pltpu.VMEM((n_blk, Q_TILE), F32),
        ],
        compiler_params=_params("arbitrary", "arbitrary", "arbitrary"),
        name="nsa_attn",
    )(qt, kc, vct, ovt, ksw, vswt, gbrt)


def _merge_kernel(o_ref, bch_ref, halo_ref, gmix_ref, x_ref, cw_ref, wa_ref, wc_ref, wo_ref, h_ref, *, seq):
    i = pl.program_id(0)
    bch = bch_ref[...].astype(F32)
    b_gate, u = bch[:, :CONV_DIM], bch[:, CONV_DIM:2 * CONV_DIM] * bch[:, 2 * CONV_DIM:]
    halo = halo_ref[...].astype(F32)
    u_halo = halo[:, CONV_DIM:2 * CONV_DIM] * halo[:, 2 * CONV_DIM:]
    u_halo = jnp.where((i * ROW_TILE) % seq == 0, 0.0, u_halo)
    row = lax.broadcasted_iota(jnp.int32, (ROW_TILE, 1), 0)
    u1 = jnp.where(row == 0, u_halo[7:8], pltpu.roll(u, 1, 0))
    u2 = jnp.where(row == 0, u_halo[6:7], jnp.where(row == 1, u_halo[7:8], pltpu.roll(u, 2, 0)))
    cw = cw_ref[...]
    conv = u2 * cw[0:1] + u1 * cw[1:2] + u * cw[2:3]
    y_conv = _dot((b_gate * conv).astype(BF16), wc_ref[...])
    y_attn = _dot(o_ref[...], wa_ref[...])
    gmix = gmix_ref[...].astype(F32)
    mix = jax.nn.sigmoid(gmix[:, :D_MODEL]) * y_conv + jax.nn.sigmoid(gmix[:, D_MODEL:]) * y_attn
    h_ref[...] = x_ref[...] + _dot(mix.astype(BF16), wo_ref[...])


def _merge(o, bch, gmix, x2, conv_w, w_attn_out, w_conv_out, w_o, seq):
    t = x2.shape[0]
    row = lambda w: pl.BlockSpec((ROW_TILE, w), lambda i: (i, 0))
    halo_blocks = ROW_TILE // 8
    kern = functools.partial(_merge_kernel, seq=seq)
    return pl.pallas_call(
        kern,
        grid=(t // ROW_TILE,),
        in_specs=[
            row(Q_COLS), row(3 * CONV_DIM),
            pl.BlockSpec((8, 3 * CONV_DIM), lambda i: (jnp.maximum(i * halo_blocks - 1, 0), 0)),
            row(2 * D_MODEL), row(D_MODEL),
            _resident((CONV_WIDTH, CONV_DIM)),
            _resident((Q_COLS, D_MODEL)), _resident((CONV_DIM, D_MODEL)), _resident((D_MODEL, D_MODEL)),
        ],
        out_specs=row(D_MODEL),
        out_shape=jax.ShapeDtypeStruct((t, D_MODEL), F32),
        compiler_params=_params("arbitrary"),
        name="merge",
    )(o, bch, bch, gmix, x2, conv_w, w_attn_out, w_conv_out, w_o)


FF_CHUNK = 256


def _ffn_kernel(h_ref, gf_ref, gl_ref, wg_ref, wu_ref, wd_ref, out_ref, act_ref):
    h = h_ref[...]
    n = _rmsnorm(h, gf_ref[...]).astype(BF16)
    for c0 in range(0, D_FF, FF_CHUNK):
        cs = slice(c0, c0 + FF_CHUNK)
        act_ref[:, cs] = (jax.nn.silu(_dot(n, wg_ref[:, cs])) * _dot(n, wu_ref[:, cs])).astype(BF16)
    h = h + _dot(act_ref[...], wd_ref[...])
    out_ref[...] = _rmsnorm(h, gl_ref[...])


def _ffn(h1, g_ffn, g_final, w_gate, w_up, w_down):
    t = h1.shape[0]
    row = pl.BlockSpec((ROW_TILE, D_MODEL), lambda i: (i, 0))
    return pl.pallas_call(
        _ffn_kernel,
        grid=(t // ROW_TILE,),
        in_specs=[row, _resident((1, D_MODEL)), _resident((1, D_MODEL)),
                  _resident((D_MODEL, D_FF)), _resident((D_MODEL, D_FF)), _resident((D_FF, D_MODEL))],
        out_specs=row,
        out_shape=jax.ShapeDtypeStruct((t, D_MODEL), F32),
        scratch_shapes=[pltpu.VMEM((ROW_TILE, D_FF), BF16)],
        compiler_params=_params("arbitrary"),
        name="ffn",
    )(h1, g_ffn, g_final, w_gate, w_up, w_down)


def _overlap_t(n_chunk, n_sel, n_blk):
    cs = np.arange(n_chunk)[None, :] * CMP_STRIDE
    ss = np.arange(n_blk)[:, None] * SEL_LEN
    ov = np.maximum(0, np.minimum(cs + CMP_LEN, ss + SEL_LEN) - np.maximum(cs, ss)) / CMP_LEN
    ov[n_sel:, :] = 0
    ov[:, n_chunk - 1] = 0
    return jnp.asarray(ov, dtype=BF16)


def _tile_heads(w1):
    pair = LANES // HEAD_DIM
    w1 = w1.reshape(CMP_LEN, 1, HEAD_DIM, CMP_HIDDEN)
    return jnp.broadcast_to(w1, (CMP_LEN, pair, HEAD_DIM, CMP_HIDDEN)).reshape(CMP_LEN, LANES, CMP_HIDDEN)


def kernel(x, w_in, conv_w, w_conv_out, cmp_pos_k, cmp_w1_k, cmp_w2_k, cmp_pos_v, cmp_w1_v, cmp_w2_v,
           w_attn_out, w_o, g_mix, g_ffn, w_gate, w_up, w_down, g_final):
    batch, seq, _ = x.shape
    assert w_in.shape[0] == 1, "one layer"
    assert seq % Q_TILE == 0 and (batch * seq) % ROW_TILE == 0 and seq % ROW_TILE == 0
    n_sel = seq // SEL_LEN
    assert n_sel <= SEL_LEN, "the block one-hot occupies SEL_LEN lanes"
    n_top = min(SEL_TOP, n_sel)
    n_chunk = seq // CMP_STRIDE
    n_blk = -(-n_sel // 16) * 16

    x2 = x.reshape(batch * seq, D_MODEL)
    w_row, w_col = _arrange_w_in(w_in[0])
    bch, kvc, ksw, gmix, qt, vswt, gbrt = _in_proj(x2, g_mix[0][None], w_row, w_col)

    pair = LANES // HEAD_DIM
    pos2 = jnp.stack([jnp.tile(cmp_pos_k[0], (1, pair)), jnp.tile(cmp_pos_v[0], (1, pair))])
    w1t = jnp.stack([_tile_heads(cmp_w1_k[0]), _tile_heads(cmp_w1_v[0])]).astype(BF16)
    kc, vct = _compress(kvc, pos2, w1t, cmp_w2_k[0].astype(BF16), cmp_w2_v[0].T.astype(BF16), batch, seq)

    o = _nsa_attn(qt, kc, vct, _overlap_t(n_chunk, n_sel, n_blk), ksw, vswt, gbrt, batch, seq, n_sel, n_top)

    h1 = _merge(o, bch, gmix, x2, conv_w[0], w_attn_out[0].astype(BF16), w_conv_out[0].astype(BF16),
                w_o[0].astype(BF16), seq)
    out = _ffn(h1, g_ffn[0][None], g_final[None], w_gate[0].astype(BF16), w_up[0].astype(BF16),
               w_down[0].astype(BF16))
    return out.reshape(batch, seq, D_MODEL)
```

```python
import functools

import numpy as np
import jax
import jax.numpy as jnp
from jax import lax
from jax.experimental import pallas as pl
from jax.experimental.pallas import tpu as pltpu

D_MODEL = 1024
CONV_DIM = 512
CONV_WIDTH = 3
N_HEADS = 16
N_KV_HEADS = 4
HEAD_DIM = 64
GROUP = N_HEADS // N_KV_HEADS
CMP_LEN = 32
CMP_STRIDE = 16
CMP_HIDDEN = 256
SEL_LEN = 64
SEL_TOP = 16
WINDOW = 512
N_BRANCH = 3
D_FF = 2816
EPS = 1e-6

Q_COLS = N_HEADS * HEAD_DIM
KV_COLS = N_KV_HEADS * HEAD_DIM
GQ_COLS = GROUP * HEAD_DIM
GATE_ROWS = 16
LANES = 128
VMEM_LIMIT = 56 * 1024 * 1024

ROW_TILE = 512
Q_TILE = 256
K_TILE = 256

NEG = -(2.0 ** 100)
RANK_FORCED = 8.0
RANK_INVALID = -1.0

BF16 = jnp.bfloat16
F32 = jnp.float32


def _dot(a, b):
    return jnp.dot(a, b, preferred_element_type=F32)


def _dot_nt(a, b):
    return lax.dot_general(a, b, (((1,), (1,)), ((), ())), preferred_element_type=F32)


def _rmsnorm(x, g):
    y = x * lax.rsqrt(jnp.mean(x * x, axis=-1, keepdims=True) + EPS)
    return y * g


def _params(*semantics):
    return pltpu.CompilerParams(dimension_semantics=semantics, vmem_limit_bytes=VMEM_LIMIT)


def _resident(shape):
    zeros = (0,) * len(shape)
    return pl.BlockSpec(shape, lambda *_: zeros, pipeline_mode=pl.Buffered(1))


ROW_GROUPS = (
    ("bch", 3 * CONV_DIM, BF16),
    ("kvc", 2 * KV_COLS, F32),
    ("ksw", 2 * KV_COLS, BF16),
    ("gmix", 2 * D_MODEL, BF16),
)
COL_GROUPS = (
    ("qt", Q_COLS, BF16),
    ("vswt", 2 * KV_COLS, BF16),
    ("gbrt", N_KV_HEADS * GATE_ROWS, F32),
)
ROW_WIDTH = sum(w for _, w, _ in ROW_GROUPS)
COL_HEIGHT = sum(w for _, w, _ in COL_GROUPS)


def _arrange_w_in(w_in):
    sizes = (CONV_DIM, CONV_DIM, CONV_DIM, Q_COLS) + (KV_COLS,) * 6 + (N_BRANCH * N_HEADS, D_MODEL, D_MODEL)
    pts = np.cumsum(sizes)[:-1]
    (b, c, h, q, k_c, v_c, k_s, v_s, k_w, v_w, g_br, g_conv, g_attn) = jnp.split(w_in, pts, axis=-1)

    def per_head(a, b_):
        a = a.reshape(D_MODEL, N_KV_HEADS, HEAD_DIM)
        b_ = b_.reshape(D_MODEL, N_KV_HEADS, HEAD_DIM)
        return jnp.concatenate([a, b_], axis=-1).reshape(D_MODEL, 2 * KV_COLS)

    g_br = g_br.reshape(D_MODEL, N_BRANCH, N_KV_HEADS, GROUP).transpose(0, 2, 1, 3)
    g_br = g_br.reshape(D_MODEL, N_KV_HEADS, N_BRANCH * GROUP)
    g_br = jnp.pad(g_br, ((0, 0), (0, 0), (0, GATE_ROWS - N_BRANCH * GROUP))).reshape(D_MODEL, -1)
    w_row = jnp.concatenate([b, c, h, k_c, v_c, per_head(k_s, k_w), g_conv, g_attn], axis=-1)
    w_col = jnp.concatenate([q * (HEAD_DIM ** -0.5), per_head(v_s, v_w), g_br], axis=-1)
    return w_row.astype(BF16), w_col.T.astype(BF16)


def _in_proj_kernel(x_ref, g_ref, wr_ref, wc_ref, *out_refs):
    n = _rmsnorm(x_ref[...], g_ref[...]).astype(BF16)
    row_refs, col_refs = out_refs[:len(ROW_GROUPS)], out_refs[len(ROW_GROUPS):]
    off = 0
    for ref, (_, width, _) in zip(row_refs, ROW_GROUPS):
        for c0 in range(0, width, 512):
            cw = min(512, width - c0)
            ref[:, c0:c0 + cw] = _dot(n, wr_ref[:, off + c0:off + c0 + cw]).astype(ref.dtype)
        off += width
    off = 0
    for ref, (_, height, _) in zip(col_refs, COL_GROUPS):
        for r0 in range(0, height, 512):
            rh = min(512, height - r0)
            ref[r0:r0 + rh, :] = _dot_nt(wc_ref[off + r0:off + r0 + rh, :], n).astype(ref.dtype)
        off += height


def _in_proj(x2, g_mix, w_row, w_col):
    t = x2.shape[0]
    row = lambda w: pl.BlockSpec((ROW_TILE, w), lambda i: (i, 0))
    col = lambda h: pl.BlockSpec((h, ROW_TILE), lambda i: (0, i))
    return pl.pallas_call(
        _in_proj_kernel,
        grid=(t // ROW_TILE,),
        in_specs=[row(D_MODEL), _resident((1, D_MODEL)), _resident((D_MODEL, ROW_WIDTH)),
                  _resident((COL_HEIGHT, D_MODEL))],
        out_specs=[row(w) for _, w, _ in ROW_GROUPS] + [col(h) for _, h, _ in COL_GROUPS],
        out_shape=[jax.ShapeDtypeStruct((t, w), dt) for _, w, dt in ROW_GROUPS]
        + [jax.ShapeDtypeStruct((h, t), dt) for _, h, dt in COL_GROUPS],
        compiler_params=_params("arbitrary"),
        name="in_proj",
    )(x2, g_mix, w_row, w_col)


def _compress_kernel(xk_ref, xv_ref, pos_ref, w1_ref, w2k_ref, w2vt_ref, kc_ref, vct_ref, *, n_chunk):
    lane = lax.broadcasted_iota(jnp.int32, (n_chunk, LANES), 1)
    half = CMP_LEN // 2
    for kv, x_ref in enumerate((xk_ref, xv_ref)):
        for h in range(LANES // HEAD_DIM):
            in_head = (lane >= h * HEAD_DIM) & (lane < (h + 1) * HEAD_DIM)
            top = jnp.zeros((n_chunk, CMP_HIDDEN), F32)
            bot = jnp.zeros((n_chunk, CMP_HIDDEN), F32)
            for l in range(half):
                x = x_ref[pl.ds(l, n_chunk, stride=CMP_STRIDE), :]
                xt = jnp.where(in_head, x + pos_ref[kv, l:l + 1, :], 0.0).astype(BF16)
                xb = jnp.where(in_head, x + pos_ref[kv, half + l:half + l + 1, :], 0.0).astype(BF16)
                top = top + _dot(xt, w1_ref[kv, l])
                bot = bot + _dot(xb, w1_ref[kv, half + l])
            pre = top + pltpu.roll(bot, n_chunk - 1, 0)
            hid = jax.nn.gelu(pre).astype(BF16)
            if kv == 0:
                kc_ref[h] = _dot(hid, w2k_ref[...]).astype(BF16)
            else:
                vct_ref[h] = _dot_nt(w2vt_ref[...], hid).astype(BF16)


def _compress(kvc, pos2, w1t, w2k, w2vt, batch, seq):
    n_chunk = seq // CMP_STRIDE
    pair = LANES // HEAD_DIM
    n_pair = N_KV_HEADS // pair
    kern = functools.partial(_compress_kernel, n_chunk=n_chunk)
    return pl.pallas_call(
        kern,
        grid=(batch, n_pair),
        in_specs=[
            pl.BlockSpec((seq, LANES), lambda b, hp: (b, hp)),
            pl.BlockSpec((seq, LANES), lambda b, hp: (b, n_pair + hp)),
            _resident((2, CMP_LEN, LANES)),
            _resident((2, CMP_LEN, LANES, CMP_HIDDEN)),
            _resident((CMP_HIDDEN, HEAD_DIM)),
            _resident((HEAD_DIM, CMP_HIDDEN)),
        ],
        out_specs=[
            pl.BlockSpec((None, pair, n_chunk, HEAD_DIM), lambda b, hp: (b, hp, 0, 0)),
            pl.BlockSpec((None, pair, HEAD_DIM, n_chunk), lambda b, hp: (b, hp, 0, 0)),
        ],
        out_shape=[
            jax.ShapeDtypeStruct((batch, N_KV_HEADS, n_chunk, HEAD_DIM), BF16),
            jax.ShapeDtypeStruct((batch, N_KV_HEADS, HEAD_DIM, n_chunk), BF16),
        ],
        compiler_params=_params("arbitrary", "arbitrary"),
        name="compress",
    )(kvc, kvc, pos2, w1t, w2k, w2vt)


def _nsa_attn_kernel(qt_ref, kc_ref, vct_ref, ovt_ref, ksw_ref, vswt_ref, gbrt_ref, o_ref,
                     ka_ref, kw_ref, vat_ref, vwt_ref, qa_ref, m_ref, acc_ref, rank_ref, s_ref,
                     *, seq, n_sel, n_top):
    i = pl.program_id(2)
    q0 = i * Q_TILE
    lanes = GROUP * Q_TILE
    n_chunk = seq // CMP_STRIDE
    n_blk = ovt_ref.shape[0]

    @pl.when(i == 0)
    def _stage_kv():
        blk_of_key = lax.broadcasted_iota(jnp.int32, (seq, HEAD_DIM), 0) // SEL_LEN
        lane = lax.broadcasted_iota(jnp.int32, (seq, HEAD_DIM), 1)
        onehot = jnp.where(blk_of_key == lane, 1.0, 0.0).astype(BF16)
        ksw = ksw_ref[...]
        ka_ref[...] = jnp.concatenate([ksw[:, :HEAD_DIM], onehot], axis=1)
        kw_ref[...] = jnp.concatenate([ksw[:, HEAD_DIM:], jnp.zeros((seq, HEAD_DIM), BF16)], axis=1)
        ones = jnp.ones((HEAD_DIM, K_TILE), BF16)
        for kt in range(seq // K_TILE):
            cs = slice(kt * K_TILE, (kt + 1) * K_TILE)
            vat_ref[kt] = jnp.concatenate([vswt_ref[:HEAD_DIM, cs], ones], axis=0)
            vwt_ref[kt] = jnp.concatenate([vswt_ref[HEAD_DIM:, cs], ones], axis=0)

    t_row = q0 + lax.broadcasted_iota(jnp.int32, (1, Q_TILE), 1)
    n_col = lax.broadcasted_iota(jnp.int32, (n_chunk, 1), 0)
    cmp_ok = (n_col * CMP_STRIDE + (CMP_LEN - 1) <= t_row) & (n_col < n_chunk - 1)
    kc = kc_ref[...]
    vct = vct_ref[...]
    p_sum = jnp.zeros((n_chunk, Q_TILE), F32)
    o_cmp = []
    for g in range(GROUP):
        s = jnp.where(cmp_ok, _dot(kc, qt_ref[g * HEAD_DIM:(g + 1) * HEAD_DIM, :]), -jnp.inf)
        m = jnp.max(s, axis=0, keepdims=True)
        m = jnp.where(jnp.isfinite(m), m, 0.0)
        e = jnp.exp(s - m)
        p = e / jnp.maximum(jnp.sum(e, axis=0, keepdims=True), 1e-30)
        o_cmp.append(_dot(vct, p.astype(BF16)))
        p_sum = p_sum + p

    ovt = ovt_ref[...]
    p1 = p_sum.astype(BF16)
    r1 = p_sum - p1.astype(F32)
    p2 = r1.astype(BF16)
    p3 = (r1 - p2.astype(F32)).astype(BF16)
    imp = _dot(ovt, p1) + _dot(ovt, p2) + _dot(ovt, p3)
    j_col = lax.broadcasted_iota(jnp.int32, (n_blk, 1), 0)
    cur = t_row // SEL_LEN
    forced = (j_col == 0) | (j_col == cur) | (j_col == cur - 1)
    valid = (j_col * SEL_LEN <= t_row) & (j_col < n_sel)
    v = jnp.where(valid, jnp.where(forced, RANK_FORCED, imp), RANK_INVALID)
    rank_ref[...] = v
    rank = jnp.zeros((n_blk, Q_TILE), jnp.int32)
    for jp in range(n_sel):
        row = rank_ref[jp:jp + 1, :]
        rank = rank + jnp.where(j_col > jp, jnp.where(row >= v, 1, 0), jnp.where(row > v, 1, 0))
    sel = (rank < n_top) & valid
    bias = jnp.where(sel, 0.0, NEG).astype(BF16)
    if n_blk < SEL_LEN:
        bias = jnp.concatenate([bias, jnp.full((SEL_LEN - n_blk, Q_TILE), NEG, BF16)], axis=0)
    for g in range(GROUP):
        cs = slice(g * Q_TILE, (g + 1) * Q_TILE)
        qa_ref[:HEAD_DIM, cs] = qt_ref[g * HEAD_DIM:(g + 1) * HEAD_DIM, :]
        qa_ref[HEAD_DIM:, cs] = bias

    r_key = lax.broadcasted_iota(jnp.int32, (K_TILE, 1), 0)
    c_tok = lax.broadcasted_iota(jnp.int32, (1, lanes), 1) % Q_TILE

    causal = r_key <= c_tok

    def scores(k_ref, kt):
        k0 = pl.multiple_of(kt * K_TILE, K_TILE)
        return _dot(k_ref[pl.ds(k0, K_TILE), :], qa_ref[...])

    def update(s, vt_ref, kt):
        m_old = m_ref[...]
        m_new = jnp.maximum(m_old, jnp.max(s, axis=0, keepdims=True))
        p = jnp.exp(s - m_new).astype(BF16)
        acc_ref[...] = jnp.exp(m_old - m_new) * acc_ref[...] + _dot(vt_ref[kt], p)
        m_ref[...] = m_new

    def finish():
        acc = acc_ref[...]
        return acc[:HEAD_DIM] / jnp.maximum(acc[HEAD_DIM:HEAD_DIM + 1], 1e-30)

    m_ref[...] = jnp.full((1, lanes), -jnp.inf, F32)
    acc_ref[...] = jnp.zeros((LANES, lanes), F32)
    s_ref[0] = jnp.where(causal, scores(ka_ref, i), NEG)
    s_ref[1] = scores(ka_ref, 0)
    update(s_ref[0], vat_ref, i)

    def pair(j, carry):
        s_ref[0] = scores(ka_ref, 2 * j + 1)
        update(s_ref[1], vat_ref, 2 * j)
        s_ref[1] = scores(ka_ref, jnp.minimum(2 * j + 2, i - 1))
        update(s_ref[0], vat_ref, 2 * j + 1)
        return carry

    lax.fori_loop(0, i // 2, pair, 0)

    @pl.when(i % 2 == 1)
    def _():
        update(s_ref[1], vat_ref, i - 1)

    o_sel = finish()

    @pl.when(i >= 2)
    def _():
        s_edge = jnp.where(causal, scores(kw_ref, i), scores(kw_ref, i - 2))
        s_mid = scores(kw_ref, i - 1)
        m = jnp.maximum(jnp.max(s_edge, axis=0, keepdims=True), jnp.max(s_mid, axis=0, keepdims=True))
        p_edge = jnp.exp(s_edge - m)
        p_mid = jnp.exp(s_mid - m).astype(BF16)
        acc_ref[...] = (_dot(vwt_ref[i], jnp.where(causal, p_edge, 0.0).astype(BF16))
                        + _dot(vwt_ref[i - 2], jnp.where(causal, 0.0, p_edge).astype(BF16))
                        + _dot(vwt_ref[i - 1], p_mid))

    @pl.when(i < 2)
    def _():
        m_ref[...] = jnp.full((1, lanes), -jnp.inf, F32)
        acc_ref[...] = jnp.zeros((LANES, lanes), F32)
        update(jnp.where(causal, scores(kw_ref, i), NEG), vwt_ref, i)

        @pl.when(i == 1)
        def _():
            update(scores(kw_ref, 0), vwt_ref, 0)

    o_win = finish()

    gate = jax.nn.sigmoid(gbrt_ref[...])
    heads = []
    for g in range(GROUP):
        cs = slice(g * Q_TILE, (g + 1) * Q_TILE)
        heads.append(gate[g:g + 1] * o_cmp[g]
                     + gate[GROUP + g:GROUP + g + 1] * o_sel[:, cs]
                     + gate[2 * GROUP + g:2 * GROUP + g + 1] * o_win[:, cs])
    o_ref[...] = jnp.concatenate(heads, axis=0).T.astype(o_ref.dtype)


def _nsa_attn(qt, kc, vct, ovt, ksw, vswt, gbrt, batch, seq, n_sel, n_top):
    nq = seq // Q_TILE
    nkt = seq // K_TILE
    lanes = GROUP * Q_TILE
    n_chunk = seq // CMP_STRIDE
    n_blk = ovt.shape[0]
    kern = functools.partial(_nsa_attn_kernel, seq=seq, n_sel=n_sel, n_top=n_top)
    return pl.pallas_call(
        kern,
        grid=(batch, N_KV_HEADS, nq),
        in_specs=[
            pl.BlockSpec((GQ_COLS, Q_TILE), lambda b, h, i: (h, b * nq + i)),
            pl.BlockSpec((None, None, n_chunk, HEAD_DIM), lambda b, h, i: (b, h, 0, 0)),
            pl.BlockSpec((None, None, HEAD_DIM, n_chunk), lambda b, h, i: (b, h, 0, 0)),
            pl.BlockSpec((n_blk, n_chunk), lambda b, h, i: (0, 0)),
            pl.BlockSpec((seq, 2 * HEAD_DIM), lambda b, h, i: (b, h)),
            pl.BlockSpec((2 * HEAD_DIM, seq), lambda b, h, i: (h, b)),
            pl.BlockSpec((GATE_ROWS, Q_TILE), lambda b, h, i: (h, b * nq + i)),
        ],
        out_specs=pl.BlockSpec((Q_TILE, GQ_COLS), lambda b, h, i: (b * nq + i, h)),
        out_shape=jax.ShapeDtypeStruct((batch * seq, Q_COLS), BF16),
        scratch_shapes=[
            pltpu.VMEM((seq, LANES), BF16),
            pltpu.VMEM((seq, LANES), BF16),
            pltpu.VMEM((nkt, LANES, K_TILE), BF16),
            pltpu.VMEM((nkt, LANES, K_TILE), BF16),
            pltpu.VMEM((LANES, lanes), BF16),
            pltpu.VMEM((1, lanes), F32),
            pltpu.VMEM((LANES, lanes), F32),
            pltpu.VMEM((n_blk, Q_TILE), F32),
            pltpu.VMEM((2, K_TILE, lanes), F32),
        ],
        compiler_params=_params("arbitrary", "arbitrary", "arbitrary"),
        name="nsa_attn",
    )(qt, kc, vct, ovt, ksw, vswt, gbrt)


def _merge_kernel(o_ref, bch_ref, halo_ref, gmix_ref, x_ref, cw_ref, wa_ref, wc_ref, wo_ref, h_ref, *, seq):
    i = pl.program_id(0)
    bch = bch_ref[...].astype(F32)
    b_gate, u = bch[:, :CONV_DIM], bch[:, CONV_DIM:2 * CONV_DIM] * bch[:, 2 * CONV_DIM:]
    halo = halo_ref[...].astype(F32)
    u_halo = halo[:, CONV_DIM:2 * CONV_DIM] * halo[:, 2 * CONV_DIM:]
    u_halo = jnp.where((i * ROW_TILE) % seq == 0, 0.0, u_halo)
    row = lax.broadcasted_iota(jnp.int32, (ROW_TILE, 1), 0)
    u1 = jnp.where(row == 0, u_halo[7:8], pltpu.roll(u, 1, 0))
    u2 = jnp.where(row == 0, u_halo[6:7], jnp.where(row == 1, u_halo[7:8], pltpu.roll(u, 2, 0)))
    cw = cw_ref[...]
    conv = u2 * cw[0:1] + u1 * cw[1:2] + u * cw[2:3]
    y_conv = _dot((b_gate * conv).astype(BF16), wc_ref[...])
    y_attn = _dot(o_ref[...], wa_ref[...])
    gmix = gmix_ref[...].astype(F32)
    mix = jax.nn.sigmoid(gmix[:, :D_MODEL]) * y_conv + jax.nn.sigmoid(gmix[:, D_MODEL:]) * y_attn
    h_ref[...] = x_ref[...] + _dot(mix.astype(BF16), wo_ref[...])


def _merge(o, bch, gmix, x2, conv_w, w_attn_out, w_conv_out, w_o, seq):
    t = x2.shape[0]
    row = lambda w: pl.BlockSpec((ROW_TILE, w), lambda i: (i, 0))
    halo_blocks = ROW_TILE // 8
    kern = functools.partial(_merge_kernel, seq=seq)
    return pl.pallas_call(
        kern,
        grid=(t // ROW_TILE,),
        in_specs=[
            row(Q_COLS), row(3 * CONV_DIM),
            pl.BlockSpec((8, 3 * CONV_DIM), lambda i: (jnp.maximum(i * halo_blocks - 1, 0), 0)),
            row(2 * D_MODEL), row(D_MODEL),
            _resident((CONV_WIDTH, CONV_DIM)),
            _resident((Q_COLS, D_MODEL)), _resident((CONV_DIM, D_MODEL)), _resident((D_MODEL, D_MODEL)),
        ],
        out_specs=row(D_MODEL),
        out_shape=jax.ShapeDtypeStruct((t, D_MODEL), F32),
        compiler_params=_params("arbitrary"),
        name="merge",
    )(o, bch, bch, gmix, x2, conv_w, w_attn_out, w_conv_out, w_o)


FF_CHUNK = 256


def _ffn_kernel(h_ref, gf_ref, gl_ref, wg_ref, wu_ref, wd_ref, out_ref, act_ref):
    h = h_ref[...]
    n = _rmsnorm(h, gf_ref[...]).astype(BF16)
    for c0 in range(0, D_FF, FF_CHUNK):
        cs = slice(c0, c0 + FF_CHUNK)
        act_ref[:, cs] = (jax.nn.silu(_dot(n, wg_ref[:, cs])) * _dot(n, wu_ref[:, cs])).astype(BF16)
    h = h + _dot(act_ref[...], wd_ref[...])
    out_ref[...] = _rmsnorm(h, gl_ref[...])


def _ffn(h1, g_ffn, g_final, w_gate, w_up, w_down):
    t = h1.shape[0]
    row = pl.BlockSpec((ROW_TILE, D_MODEL), lambda i: (i, 0))
    return pl.pallas_call(
        _ffn_kernel,
        grid=(t // ROW_TILE,),
        in_specs=[row, _resident((1, D_MODEL)), _resident((1, D_MODEL)),
                  _resident((D_MODEL, D_FF)), _resident((D_MODEL, D_FF)), _resident((D_FF, D_MODEL))],
        out_specs=row,
        out_shape=jax.ShapeDtypeStruct((t, D_MODEL), F32),
        scratch_shapes=[pltpu.VMEM((ROW_TILE, D_FF), BF16)],
        compiler_params=_params("arbitrary"),
        name="ffn",
    )(h1, g_ffn, g_final, w_gate, w_up, w_down)


def _overlap_t(n_chunk, n_sel, n_blk):
    cs = np.arange(n_chunk)[None, :] * CMP_STRIDE
    ss = np.arange(n_blk)[:, None] * SEL_LEN
    ov = np.maximum(0, np.minimum(cs + CMP_LEN, ss + SEL_LEN) - np.maximum(cs, ss)) / CMP_LEN
    ov[n_sel:, :] = 0
    ov[:, n_chunk - 1] = 0
    return jnp.asarray(ov, dtype=BF16)


def _tile_heads(w1):
    pair = LANES // HEAD_DIM
    w1 = w1.reshape(CMP_LEN, 1, HEAD_DIM, CMP_HIDDEN)
    return jnp.broadcast_to(w1, (CMP_LEN, pair, HEAD_DIM, CMP_HIDDEN)).reshape(CMP_LEN, LANES, CMP_HIDDEN)


def kernel(x, w_in, conv_w, w_conv_out, cmp_pos_k, cmp_w1_k, cmp_w2_k, cmp_pos_v, cmp_w1_v, cmp_w2_v,
           w_attn_out, w_o, g_mix, g_ffn, w_gate, w_up, w_down, g_final):
    batch, seq, _ = x.shape
    assert w_in.shape[0] == 1, "one layer"
    assert seq % Q_TILE == 0 and (batch * seq) % ROW_TILE == 0 and seq % ROW_TILE == 0
    n_sel = seq // SEL_LEN
    assert n_sel <= SEL_LEN, "the block one-hot occupies SEL_LEN lanes"
    n_top = min(SEL_TOP, n_sel)
    n_chunk = seq // CMP_STRIDE
    n_blk = -(-n_sel // 16) * 16

    x2 = x.reshape(batch * seq, D_MODEL)
    w_row, w_col = _arrange_w_in(w_in[0])
    bch, kvc, ksw, gmix, qt, vswt, gbrt = _in_proj(x2, g_mix[0][None], w_row, w_col)

    pair = LANES // HEAD_DIM
    pos2 = jnp.stack([jnp.tile(cmp_pos_k[0], (1, pair)), jnp.tile(cmp_pos_v[0], (1, pair))])
    w1t = jnp.stack([_tile_heads(cmp_w1_k[0]), _tile_heads(cmp_w1_v[0])]).astype(BF16)
    kc, vct = _compress(kvc, pos2, w1t, cmp_w2_k[0].astype(BF16), cmp_w2_v[0].T.astype(BF16), batch, seq)

    o = _nsa_attn(qt, kc, vct, _overlap_t(n_chunk, n_sel, n_blk), ksw, vswt, gbrt, batch, seq, n_sel, n_top)

    h1 = _merge(o, bch, gmix, x2, conv_w[0], w_attn_out[0].astype(BF16), w_conv_out[0].astype(BF16),
                w_o[0].astype(BF16), seq)
    out = _ffn(h1, g_ffn[0][None], g_final[None], w_gate[0].astype(BF16), w_up[0].astype(BF16),
               w_down[0].astype(BF16))
    return out.reshape(batch, seq, D_MODEL)
```

```python
import functools

import numpy as np
import jax
import jax.numpy as jnp
from jax import lax
from jax.experimental import pallas as pl
from jax.experimental.pallas import tpu as pltpu

D_MODEL = 1024
CONV_DIM = 512
CONV_WIDTH = 3
N_HEADS = 16
N_KV_HEADS = 4
HEAD_DIM = 64
GROUP = N_HEADS // N_KV_HEADS
CMP_LEN = 32
CMP_STRIDE = 16
CMP_HIDDEN = 256
SEL_LEN = 64
SEL_TOP = 16
WINDOW = 512
N_BRANCH = 3
D_FF = 2816
EPS = 1e-6

Q_COLS = N_HEADS * HEAD_DIM
KV_COLS = N_KV_HEADS * HEAD_DIM
GQ_COLS = GROUP * HEAD_DIM
GATE_ROWS = 16
LANES = 128
VMEM_LIMIT = 56 * 1024 * 1024

ROW_TILE = 512
Q_TILE = 256
K_TILE = 256
CLASS_TILES = 4

Q_SCALE = HEAD_DIM ** -0.5 * 1.4426950408889634

NEG = -(2.0 ** 100)
RANK_FORCED = 8.0
RANK_INVALID = -1.0

BF16 = jnp.bfloat16
F32 = jnp.float32


def _dot(a, b):
    return jnp.dot(a, b, preferred_element_type=F32)


def _dot_nt(a, b):
    return lax.dot_general(a, b, (((1,), (1,)), ((), ())), preferred_element_type=F32)


def _rmsnorm(x, g):
    y = x * lax.rsqrt(jnp.mean(x * x, axis=-1, keepdims=True) + EPS)
    return y * g


def _params(*semantics):
    return pltpu.CompilerParams(dimension_semantics=semantics, vmem_limit_bytes=VMEM_LIMIT)


def _resident(shape):
    zeros = (0,) * len(shape)
    return pl.BlockSpec(shape, lambda *_: zeros, pipeline_mode=pl.Buffered(1))


ROW_GROUPS = (
    ("bch", 3 * CONV_DIM, BF16),
    ("kvc", 2 * KV_COLS, F32),
    ("ksw", 2 * KV_COLS, BF16),
    ("gmix", 2 * D_MODEL, BF16),
)
COL_GROUPS = (
    ("qt", Q_COLS, BF16),
    ("vswt", 2 * KV_COLS, BF16),
    ("gbrt", N_KV_HEADS * GATE_ROWS, F32),
)
ROW_WIDTH = sum(w for _, w, _ in ROW_GROUPS)
COL_HEIGHT = sum(w for _, w, _ in COL_GROUPS)


def _arrange_w_in(w_in):
    sizes = (CONV_DIM, CONV_DIM, CONV_DIM, Q_COLS) + (KV_COLS,) * 6 + (N_BRANCH * N_HEADS, D_MODEL, D_MODEL)
    pts = np.cumsum(sizes)[:-1]
    (b, c, h, q, k_c, v_c, k_s, v_s, k_w, v_w, g_br, g_conv, g_attn) = jnp.split(w_in, pts, axis=-1)

    def per_head(a, b_):
        a = a.reshape(D_MODEL, N_KV_HEADS, HEAD_DIM)
        b_ = b_.reshape(D_MODEL, N_KV_HEADS, HEAD_DIM)
        return jnp.concatenate([a, b_], axis=-1).reshape(D_MODEL, 2 * KV_COLS)

    g_br = g_br.reshape(D_MODEL, N_BRANCH, N_KV_HEADS, GROUP).transpose(0, 2, 1, 3)
    g_br = g_br.reshape(D_MODEL, N_KV_HEADS, N_BRANCH * GROUP)
    g_br = jnp.pad(g_br, ((0, 0), (0, 0), (0, GATE_ROWS - N_BRANCH * GROUP))).reshape(D_MODEL, -1)
    w_row = jnp.concatenate([b, c, h, k_c, v_c, per_head(k_s, k_w), g_conv, g_attn], axis=-1)
    w_col = jnp.concatenate([q * Q_SCALE, per_head(v_s, v_w), g_br], axis=-1)
    return w_row.astype(BF16), w_col.T.astype(BF16)


def _in_proj_kernel(x_ref, g_ref, wr_ref, wc_ref, *out_refs):
    n = _rmsnorm(x_ref[...], g_ref[...]).astype(BF16)
    row_refs, col_refs = out_refs[:len(ROW_GROUPS)], out_refs[len(ROW_GROUPS):]
    off = 0
    for ref, (_, width, _) in zip(row_refs, ROW_GROUPS):
        for c0 in range(0, width, 512):
            cw = min(512, width - c0)
            ref[:, c0:c0 + cw] = _dot(n, wr_ref[:, off + c0:off + c0 + cw]).astype(ref.dtype)
        off += width
    off = 0
    for ref, (_, height, _) in zip(col_refs, COL_GROUPS):
        for r0 in range(0, height, 512):
            rh = min(512, height - r0)
            ref[r0:r0 + rh, :] = _dot_nt(wc_ref[off + r0:off + r0 + rh, :], n).astype(ref.dtype)
        off += height


def _in_proj(x2, g_mix, w_row, w_col):
    t = x2.shape[0]
    row = lambda w: pl.BlockSpec((ROW_TILE, w), lambda i: (i, 0))
    col = lambda h: pl.BlockSpec((h, ROW_TILE), lambda i: (0, i))
    return pl.pallas_call(
        _in_proj_kernel,
        grid=(t // ROW_TILE,),
        in_specs=[row(D_MODEL), _resident((1, D_MODEL)), _resident((D_MODEL, ROW_WIDTH)),
                  _resident((COL_HEIGHT, D_MODEL))],
        out_specs=[row(w) for _, w, _ in ROW_GROUPS] + [col(h) for _, h, _ in COL_GROUPS],
        out_shape=[jax.ShapeDtypeStruct((t, w), dt) for _, w, dt in ROW_GROUPS]
        + [jax.ShapeDtypeStruct((h, t), dt) for _, h, dt in COL_GROUPS],
        compiler_params=_params("arbitrary"),
        name="in_proj",
    )(x2, g_mix, w_row, w_col)


def _compress_kernel(xk_ref, xv_ref, pos_ref, w1_ref, w2k_ref, w2vt_ref, kc_ref, vct_ref, *, n_chunk):
    lane = lax.broadcasted_iota(jnp.int32, (n_chunk, LANES), 1)
    half = CMP_LEN // 2
    for kv, x_ref in enumerate((xk_ref, xv_ref)):
        for h in range(LANES // HEAD_DIM):
            in_head = (lane >= h * HEAD_DIM) & (lane < (h + 1) * HEAD_DIM)
            top = jnp.zeros((n_chunk, CMP_HIDDEN), F32)
            bot = jnp.zeros((n_chunk, CMP_HIDDEN), F32)
            for l in range(half):
                x = x_ref[pl.ds(l, n_chunk, stride=CMP_STRIDE), :]
                xt = jnp.where(in_head, x + pos_ref[kv, l:l + 1, :], 0.0).astype(BF16)
                xb = jnp.where(in_head, x + pos_ref[kv, half + l:half + l + 1, :], 0.0).astype(BF16)
                top = top + _dot(xt, w1_ref[kv, l])
                bot = bot + _dot(xb, w1_ref[kv, half + l])
            pre = top + pltpu.roll(bot, n_chunk - 1, 0)
            hid = jax.nn.gelu(pre).astype(BF16)
            if kv == 0:
                kc_ref[h] = _dot(hid, w2k_ref[...]).astype(BF16)
            else:
                vct_ref[h] = _dot_nt(w2vt_ref[...], hid).astype(BF16)


def _compress(kvc, pos2, w1t, w2k, w2vt, batch, seq):
    n_chunk = seq // CMP_STRIDE
    pair = LANES // HEAD_DIM
    n_pair = N_KV_HEADS // pair
    kern = functools.partial(_compress_kernel, n_chunk=n_chunk)
    return pl.pallas_call(
        kern,
        grid=(batch, n_pair),
        in_specs=[
            pl.BlockSpec((seq, LANES), lambda b, hp: (b, hp)),
            pl.BlockSpec((seq, LANES), lambda b, hp: (b, n_pair + hp)),
            _resident((2, CMP_LEN, LANES)),
            _resident((2, CMP_LEN, LANES, CMP_HIDDEN)),
            _resident((CMP_HIDDEN, HEAD_DIM)),
            _resident((HEAD_DIM, CMP_HIDDEN)),
        ],
        out_specs=[
            pl.BlockSpec((None, pair, n_chunk, HEAD_DIM), lambda b, hp: (b, hp, 0, 0)),
            pl.BlockSpec((None, pair, HEAD_DIM, n_chunk), lambda b, hp: (b, hp, 0, 0)),
        ],
        out_shape=[
            jax.ShapeDtypeStruct((batch, N_KV_HEADS, n_chunk, HEAD_DIM), BF16),
            jax.ShapeDtypeStruct((batch, N_KV_HEADS, HEAD_DIM, n_chunk), BF16),
        ],
        compiler_params=_params("arbitrary", "arbitrary"),
        name="compress",
    )(kvc, kvc, pos2, w1t, w2k, w2vt)


def _nsa_attn_kernel(qt_ref, kc_ref, vct_ref, ovt_ref, ksw_ref, vswt_ref, gbrt_ref, o_ref,
                     ka_ref, kw_ref, vat_ref, vwt_ref, qa_ref, m_ref, acc_ref, rank_ref, s_ref,
                     ocmp_ref, owin_ref,
                     *, seq, n_sel, n_top):
    i = pl.program_id(2)
    q0 = i * Q_TILE
    lanes = GROUP * Q_TILE
    n_chunk = seq // CMP_STRIDE
    n_blk = ovt_ref.shape[0]

    @pl.when(i == 0)
    def _stage_kv():
        blk_of_key = lax.broadcasted_iota(jnp.int32, (seq, HEAD_DIM), 0) // SEL_LEN
        lane = lax.broadcasted_iota(jnp.int32, (seq, HEAD_DIM), 1)
        onehot = jnp.where(blk_of_key == lane, 1.0, 0.0).astype(BF16)
        ksw = ksw_ref[...]
        ka_ref[...] = jnp.concatenate([ksw[:, :HEAD_DIM], onehot], axis=1)
        kw_ref[...] = jnp.concatenate([ksw[:, HEAD_DIM:], jnp.zeros((seq, HEAD_DIM), BF16)], axis=1)
        ones = jnp.ones((HEAD_DIM, K_TILE), BF16)
        for kt in range(seq // K_TILE):
            cs = slice(kt * K_TILE, (kt + 1) * K_TILE)
            vat_ref[kt] = jnp.concatenate([vswt_ref[:HEAD_DIM, cs], ones], axis=0)
            vwt_ref[kt] = jnp.concatenate([vswt_ref[HEAD_DIM:, cs], ones], axis=0)

    for g in range(GROUP):
        qa_ref[:HEAD_DIM, g * Q_TILE:(g + 1) * Q_TILE] = qt_ref[g * HEAD_DIM:(g + 1) * HEAD_DIM, :]

    c_tok = lax.broadcasted_iota(jnp.int32, (1, lanes), 1) % Q_TILE
    t_row = q0 + lax.broadcasted_iota(jnp.int32, (1, Q_TILE), 1)

    def cmp_and_select(nc, nb):
        n_col = lax.broadcasted_iota(jnp.int32, (nc, 1), 0)
        cmp_ok = (n_col * CMP_STRIDE + (CMP_LEN - 1) <= q0 + c_tok) & (n_col < n_chunk - 1)
        s = jnp.where(cmp_ok, _dot(kc_ref[:nc, :], qa_ref[:HEAD_DIM, :]), -jnp.inf)
        m = jnp.max(s, axis=0, keepdims=True)
        m = jnp.where(jnp.isfinite(m), m, 0.0)
        e = jnp.exp2(s - m)
        p = e / jnp.maximum(jnp.sum(e, axis=0, keepdims=True), 1e-30)
        ocmp_ref[...] = _dot(vct_ref[:, :nc], p.astype(BF16))
        p_sum = p[:, :Q_TILE]
        for g in range(1, GROUP):
            p_sum = p_sum + p[:, g * Q_TILE:(g + 1) * Q_TILE]

        ovt = ovt_ref[:nb, :nc]
        p1 = p_sum.astype(BF16)
        r1 = p_sum - p1.astype(F32)
        p2 = r1.astype(BF16)
        p3 = (r1 - p2.astype(F32)).astype(BF16)
        imp = _dot(ovt, p1) + _dot(ovt, p2) + _dot(ovt, p3)
        j_col = lax.broadcasted_iota(jnp.int32, (nb, 1), 0)
        cur = t_row // SEL_LEN
        forced = (j_col == 0) | (j_col == cur) | (j_col == cur - 1)
        valid = (j_col * SEL_LEN <= t_row) & (j_col < n_sel)
        v = jnp.where(valid, jnp.where(forced, RANK_FORCED, imp), RANK_INVALID)
        rank_ref[:nb, :] = v
        rank = jnp.zeros((nb, Q_TILE), jnp.int32)
        for jp in range(min(nb, n_sel)):
            row = rank_ref[jp:jp + 1, :]
            rank = rank + jnp.where(j_col > jp, jnp.where(row >= v, 1, 0), jnp.where(row > v, 1, 0))
        bias = jnp.where((rank < n_top) & valid, 0.0, NEG).astype(BF16)
        if nb < SEL_LEN:
            bias = jnp.concatenate([bias, jnp.full((SEL_LEN - nb, Q_TILE), NEG, BF16)], axis=0)
        for g in range(GROUP):
            qa_ref[HEAD_DIM:, g * Q_TILE:(g + 1) * Q_TILE] = bias

    n_class = -(-(seq // Q_TILE) // CLASS_TILES)
    for c in range(n_class):
        nb = min((c + 1) * CLASS_TILES * Q_TILE // SEL_LEN, n_blk)
        nc = min(-(-(c + 1) * CLASS_TILES * Q_TILE // CMP_STRIDE // LANES) * LANES, n_chunk)
        pl.when(i // CLASS_TILES == c)(functools.partial(cmp_and_select, nc, nb))

    r_key = lax.broadcasted_iota(jnp.int32, (K_TILE, 1), 0)

    causal = r_key <= c_tok

    def scores(k_ref, kt):
        k0 = pl.multiple_of(kt * K_TILE, K_TILE)
        return _dot(k_ref[pl.ds(k0, K_TILE), :], qa_ref[...])

    def update(s, vt_ref, kt):
        m_old = m_ref[...]
        m_new = jnp.maximum(m_old, jnp.max(s, axis=0, keepdims=True))
        p = jnp.exp2(s - m_new).astype(BF16)
        acc_ref[...] = jnp.exp2(m_old - m_new) * acc_ref[...] + _dot(vt_ref[kt], p)
        m_ref[...] = m_new

    def normalized(acc):
        return acc[:HEAD_DIM] / jnp.maximum(acc[HEAD_DIM:HEAD_DIM + 1], 1e-30)

    def sel_first():
        m_ref[...] = jnp.full((1, lanes), -jnp.inf, F32)
        acc_ref[...] = jnp.zeros((LANES, lanes), F32)
        s_ref[0] = jnp.where(causal, scores(ka_ref, i), NEG)
        s_ref[1] = scores(ka_ref, 0)
        update(s_ref[0], vat_ref, i)

    @pl.when(i >= 2)
    def _():
        sel_first()
        s_edge = jnp.where(causal, scores(kw_ref, i), scores(kw_ref, i - 2))
        s_mid = scores(kw_ref, i - 1)
        m = jnp.maximum(jnp.max(s_edge, axis=0, keepdims=True), jnp.max(s_mid, axis=0, keepdims=True))
        p_edge = jnp.exp2(s_edge - m)
        p_mid = jnp.exp2(s_mid - m).astype(BF16)
        owin_ref[...] = normalized(_dot(vwt_ref[i], jnp.where(causal, p_edge, 0.0).astype(BF16))
                                   + _dot(vwt_ref[i - 2], jnp.where(causal, 0.0, p_edge).astype(BF16))
                                   + _dot(vwt_ref[i - 1], p_mid))

    @pl.when(i < 2)
    def _():
        sel_first()
        s_diag = jnp.where(causal, scores(kw_ref, i), NEG)
        s_mid = jnp.where(i == 1, scores(kw_ref, 0), NEG)
        m = jnp.maximum(jnp.max(s_diag, axis=0, keepdims=True), jnp.max(s_mid, axis=0, keepdims=True))
        owin_ref[...] = normalized(_dot(vwt_ref[i], jnp.exp2(s_diag - m).astype(BF16))
                                   + _dot(vwt_ref[0], jnp.exp2(s_mid - m).astype(BF16)))

    def pair(j, carry):
        s_ref[0] = scores(ka_ref, 2 * j + 1)
        update(s_ref[1], vat_ref, 2 * j)
        s_ref[1] = scores(ka_ref, jnp.minimum(2 * j + 2, i - 1))
        update(s_ref[0], vat_ref, 2 * j + 1)
        return carry

    lax.fori_loop(0, i // 2, pair, 0)

    @pl.when(i % 2 == 1)
    def _():
        update(s_ref[1], vat_ref, i - 1)

    o_sel = normalized(acc_ref[...])
    o_cmp = ocmp_ref[...]
    o_win = owin_ref[...]
    gate = jax.nn.sigmoid(gbrt_ref[...])
    heads = []
    for g in range(GROUP):
        cs = slice(g * Q_TILE, (g + 1) * Q_TILE)
        heads.append(gate[g:g + 1] * o_cmp[:, cs]
                     + gate[GROUP + g:GROUP + g + 1] * o_sel[:, cs]
                     + gate[2 * GROUP + g:2 * GROUP + g + 1] * o_win[:, cs])
    o_ref[...] = jnp.concatenate(heads, axis=0).T.astype(o_ref.dtype)


def _nsa_attn(qt, kc, vct, ovt, ksw, vswt, gbrt, batch, seq, n_sel, n_top):
    nq = seq // Q_TILE
    nkt = seq // K_TILE
    lanes = GROUP * Q_TILE
    n_chunk = seq // CMP_STRIDE
    n_blk = ovt.shape[0]
    kern = functools.partial(_nsa_attn_kernel, seq=seq, n_sel=n_sel, n_top=n_top)
    return pl.pallas_call(
        kern,
        grid=(batch, N_KV_HEADS, nq),
        in_specs=[
            pl.BlockSpec((GQ_COLS, Q_TILE), lambda b, h, i: (h, b * nq + i)),
            pl.BlockSpec((None, None, n_chunk, HEAD_DIM), lambda b, h, i: (b, h, 0, 0)),
            pl.BlockSpec((None, None, HEAD_DIM, n_chunk), lambda b, h, i: (b, h, 0, 0)),
            pl.BlockSpec((n_blk, n_chunk), lambda b, h, i: (0, 0)),
            pl.BlockSpec((seq, 2 * HEAD_DIM), lambda b, h, i: (b, h)),
            pl.BlockSpec((2 * HEAD_DIM, seq), lambda b, h, i: (h, b)),
            pl.BlockSpec((GATE_ROWS, Q_TILE), lambda b, h, i: (h, b * nq + i)),
        ],
        out_specs=pl.BlockSpec((Q_TILE, GQ_COLS), lambda b, h, i: (b * nq + i, h)),
        out_shape=jax.ShapeDtypeStruct((batch * seq, Q_COLS), BF16),
        scratch_shapes=[
            pltpu.VMEM((seq, LANES), BF16),
            pltpu.VMEM((seq, LANES), BF16),
            pltpu.VMEM((nkt, LANES, K_TILE), BF16),
            pltpu.VMEM((nkt, LANES, K_TILE), BF16),
            pltpu.VMEM((LANES, lanes), BF16),
            pltpu.VMEM((1, lanes), F32),
            pltpu.VMEM((LANES, lanes), F32),
            pltpu.VMEM((n_blk, Q_TILE), F32),
            pltpu.VMEM((2, K_TILE, lanes), F32),
            pltpu.VMEM((HEAD_DIM, lanes), F32),
            pltpu.VMEM((HEAD_DIM, lanes), F32),
        ],
        compiler_params=_params("arbitrary", "arbitrary", "arbitrary"),
        name="nsa_attn",
    )(qt, kc, vct, ovt, ksw, vswt, gbrt)


def _merge_kernel(o_ref, bch_ref, halo_ref, gmix_ref, x_ref, cw_ref, wa_ref, wc_ref, wo_ref, h_ref, *, seq):
    i = pl.program_id(0)
    bch = bch_ref[...].astype(F32)
    b_gate, u = bch[:, :CONV_DIM], bch[:, CONV_DIM:2 * CONV_DIM] * bch[:, 2 * CONV_DIM:]
    halo = halo_ref[...].astype(F32)
    u_halo = halo[:, CONV_DIM:2 * CONV_DIM] * halo[:, 2 * CONV_DIM:]
    u_halo = jnp.where((i * ROW_TILE) % seq == 0, 0.0, u_halo)
    row = lax.broadcasted_iota(jnp.int32, (ROW_TILE, 1), 0)
    u1 = jnp.where(row == 0, u_halo[7:8], pltpu.roll(u, 1, 0))
    u2 = jnp.where(row == 0, u_halo[6:7], jnp.where(row == 1, u_halo[7:8], pltpu.roll(u, 2, 0)))
    cw = cw_ref[...]
    conv = u2 * cw[0:1] + u1 * cw[1:2] + u * cw[2:3]
    y_conv = _dot((b_gate * conv).astype(BF16), wc_ref[...])
    y_attn = _dot(o_ref[...], wa_ref[...])
    gmix = gmix_ref[...].astype(F32)
    mix = jax.nn.sigmoid(gmix[:, :D_MODEL]) * y_conv + jax.nn.sigmoid(gmix[:, D_MODEL:]) * y_attn
    h_ref[...] = x_ref[...] + _dot(mix.astype(BF16), wo_ref[...])


def _merge(o, bch, gmix, x2, conv_w, w_attn_out, w_conv_out, w_o, seq):
    t = x2.shape[0]
    row = lambda w: pl.BlockSpec((ROW_TILE, w), lambda i: (i, 0))
    halo_blocks = ROW_TILE // 8
    kern = functools.partial(_merge_kernel, seq=seq)
    return pl.pallas_call(
        kern,
        grid=(t // ROW_TILE,),
        in_specs=[
            row(Q_COLS), row(3 * CONV_DIM),
            pl.BlockSpec((8, 3 * CONV_DIM), lambda i: (jnp.maximum(i * halo_blocks - 1, 0), 0)),
            row(2 * D_MODEL), row(D_MODEL),
            _resident((CONV_WIDTH, CONV_DIM)),
            _resident((Q_COLS, D_MODEL)), _resident((CONV_DIM, D_MODEL)), _resident((D_MODEL, D_MODEL)),
        ],
        out_specs=row(D_MODEL),
        out_shape=jax.ShapeDtypeStruct((t, D_MODEL), F32),
        compiler_params=_params("arbitrary"),
        name="merge",
    )(o, bch, bch, gmix, x2, conv_w, w_attn_out, w_conv_out, w_o)


FF_CHUNK = 256


def _ffn_kernel(h_ref, gf_ref, gl_ref, wg_ref, wu_ref, wd_ref, out_ref, act_ref):
    h = h_ref[...]
    n = _rmsnorm(h, gf_ref[...]).astype(BF16)
    for c0 in range(0, D_FF, FF_CHUNK):
        cs = slice(c0, c0 + FF_CHUNK)
        act_ref[:, cs] = (jax.nn.silu(_dot(n, wg_ref[:, cs])) * _dot(n, wu_ref[:, cs])).astype(BF16)
    h = h + _dot(act_ref[...], wd_ref[...])
    out_ref[...] = _rmsnorm(h, gl_ref[...])


def _ffn(h1, g_ffn, g_final, w_gate, w_up, w_down):
    t = h1.shape[0]
    row = pl.BlockSpec((ROW_TILE, D_MODEL), lambda i: (i, 0))
    return pl.pallas_call(
        _ffn_kernel,
        grid=(t // ROW_TILE,),
        in_specs=[row, _resident((1, D_MODEL)), _resident((1, D_MODEL)),
                  _resident((D_MODEL, D_FF)), _resident((D_MODEL, D_FF)), _resident((D_FF, D_MODEL))],
        out_specs=row,
        out_shape=jax.ShapeDtypeStruct((t, D_MODEL), F32),
        scratch_shapes=[pltpu.VMEM((ROW_TILE, D_FF), BF16)],
        compiler_params=_params("arbitrary"),
        name="ffn",
    )(h1, g_ffn, g_final, w_gate, w_up, w_down)


def _overlap_t(n_chunk, n_sel, n_blk):
    cs = np.arange(n_chunk)[None, :] * CMP_STRIDE
    ss = np.arange(n_blk)[:, None] * SEL_LEN
    ov = np.maximum(0, np.minimum(cs + CMP_LEN, ss + SEL_LEN) - np.maximum(cs, ss)) / CMP_LEN
    ov[n_sel:, :] = 0
    ov[:, n_chunk - 1] = 0
    return jnp.asarray(ov, dtype=BF16)


def _tile_heads(w1):
    pair = LANES // HEAD_DIM
    w1 = w1.reshape(CMP_LEN, 1, HEAD_DIM, CMP_HIDDEN)
    return jnp.broadcast_to(w1, (CMP_LEN, pair, HEAD_DIM, CMP_HIDDEN)).reshape(CMP_LEN, LANES, CMP_HIDDEN)


def kernel(x, w_in, conv_w, w_conv_out, cmp_pos_k, cmp_w1_k, cmp_w2_k, cmp_pos_v, cmp_w1_v, cmp_w2_v,
           w_attn_out, w_o, g_mix, g_ffn, w_gate, w_up, w_down, g_final):
    batch, seq, _ = x.shape
    assert w_in.shape[0] == 1, "one layer"
    assert seq % Q_TILE == 0 and (batch * seq) % ROW_TILE == 0 and seq % ROW_TILE == 0
    n_sel = seq // SEL_LEN
    assert n_sel <= SEL_LEN, "the block one-hot occupies SEL_LEN lanes"
    n_top = min(SEL_TOP, n_sel)
    n_chunk = seq // CMP_STRIDE
    n_blk = -(-n_sel // 16) * 16

    x2 = x.reshape(batch * seq, D_MODEL)
    w_row, w_col = _arrange_w_in(w_in[0])
    bch, kvc, ksw, gmix, qt, vswt, gbrt = _in_proj(x2, g_mix[0][None], w_row, w_col)

    pair = LANES // HEAD_DIM
    pos2 = jnp.stack([jnp.tile(cmp_pos_k[0], (1, pair)), jnp.tile(cmp_pos_v[0], (1, pair))])
    w1t = jnp.stack([_tile_heads(cmp_w1_k[0]), _tile_heads(cmp_w1_v[0])]).astype(BF16)
    kc, vct = _compress(kvc, pos2, w1t, cmp_w2_k[0].astype(BF16), cmp_w2_v[0].T.astype(BF16), batch, seq)

    o = _nsa_attn(qt, kc, vct, _overlap_t(n_chunk, n_sel, n_blk), ksw, vswt, gbrt, batch, seq, n_sel, n_top)

    h1 = _merge(o, bch, gmix, x2, conv_w[0], w_attn_out[0].astype(BF16), w_conv_out[0].astype(BF16),
                w_o[0].astype(BF16), seq)
    out = _ffn(h1, g_ffn[0][None], g_final[None], w_gate[0].astype(BF16), w_up[0].astype(BF16),
               w_down[0].astype(BF16))
    return out.reshape(batch, seq, D_MODEL)
```

```python
import functools

import numpy as np
import jax
import jax.numpy as jnp
from jax import lax
from jax.experimental import pallas as pl
from jax.experimental.pallas import tpu as pltpu

D_MODEL = 1024
CONV_DIM = 512
CONV_WIDTH = 3
N_HEADS = 16
N_KV_HEADS = 4
HEAD_DIM = 64
GROUP = N_HEADS // N_KV_HEADS
CMP_LEN = 32
CMP_STRIDE = 16
CMP_HIDDEN = 256
SEL_LEN = 64
SEL_TOP = 16
WINDOW = 512
N_BRANCH = 3
D_FF = 2816
EPS = 1e-6

Q_COLS = N_HEADS * HEAD_DIM
KV_COLS = N_KV_HEADS * HEAD_DIM
GQ_COLS = GROUP * HEAD_DIM
GATE_ROWS = 16
LANES = 128
VMEM_LIMIT = 56 * 1024 * 1024

ROW_TILE = 512
Q_TILE = 256
K_TILE = 256
CLASS_TILES = 4
ONES_ROWS = 16
ACC_ROWS = HEAD_DIM + ONES_ROWS

Q_SCALE = HEAD_DIM ** -0.5 * 1.4426950408889634

NEG = -(2.0 ** 100)
RANK_FORCED = 8.0
RANK_INVALID = -1.0

BF16 = jnp.bfloat16
F32 = jnp.float32


def _dot(a, b):
    return jnp.dot(a, b, preferred_element_type=F32)


def _dot_nt(a, b):
    return lax.dot_general(a, b, (((1,), (1,)), ((), ())), preferred_element_type=F32)


def _rmsnorm(x, g):
    y = x * lax.rsqrt(jnp.mean(x * x, axis=-1, keepdims=True) + EPS)
    return y * g


def _params(*semantics):
    return pltpu.CompilerParams(dimension_semantics=semantics, vmem_limit_bytes=VMEM_LIMIT)


def _resident(shape):
    zeros = (0,) * len(shape)
    return pl.BlockSpec(shape, lambda *_: zeros, pipeline_mode=pl.Buffered(1))


ROW_GROUPS = (
    ("bch", 3 * CONV_DIM, BF16),
    ("kvc", 2 * KV_COLS, F32),
    ("ksw", 2 * KV_COLS, BF16),
    ("gmix", 2 * D_MODEL, BF16),
)
COL_GROUPS = (
    ("qt", Q_COLS, BF16),
    ("vswt", 2 * KV_COLS, BF16),
    ("gbrt", N_KV_HEADS * GATE_ROWS, F32),
)
ROW_WIDTH = sum(w for _, w, _ in ROW_GROUPS)
COL_HEIGHT = sum(w for _, w, _ in COL_GROUPS)


def _arrange_w_in(w_in):
    sizes = (CONV_DIM, CONV_DIM, CONV_DIM, Q_COLS) + (KV_COLS,) * 6 + (N_BRANCH * N_HEADS, D_MODEL, D_MODEL)
    pts = np.cumsum(sizes)[:-1]
    (b, c, h, q, k_c, v_c, k_s, v_s, k_w, v_w, g_br, g_conv, g_attn) = jnp.split(w_in, pts, axis=-1)

    def per_head(a, b_):
        a = a.reshape(D_MODEL, N_KV_HEADS, HEAD_DIM)
        b_ = b_.reshape(D_MODEL, N_KV_HEADS, HEAD_DIM)
        return jnp.concatenate([a, b_], axis=-1).reshape(D_MODEL, 2 * KV_COLS)

    g_br = g_br.reshape(D_MODEL, N_BRANCH, N_KV_HEADS, GROUP).transpose(0, 2, 1, 3)
    g_br = g_br.reshape(D_MODEL, N_KV_HEADS, N_BRANCH * GROUP)
    g_br = jnp.pad(g_br, ((0, 0), (0, 0), (0, GATE_ROWS - N_BRANCH * GROUP))).reshape(D_MODEL, -1)
    w_row = jnp.concatenate([b, c, h, k_c, v_c, per_head(k_s, k_w), g_conv, g_attn], axis=-1)
    w_col = jnp.concatenate([q * Q_SCALE, per_head(v_s, v_w), g_br], axis=-1)
    return w_row.astype(BF16), w_col.T.astype(BF16)


def _in_proj_kernel(x_ref, g_ref, wr_ref, wc_ref, *out_refs):
    n = _rmsnorm(x_ref[...], g_ref[...]).astype(BF16)
    row_refs, col_refs = out_refs[:len(ROW_GROUPS)], out_refs[len(ROW_GROUPS):]
    off = 0
    for ref, (_, width, _) in zip(row_refs, ROW_GROUPS):
        for c0 in range(0, width, 512):
            cw = min(512, width - c0)
            ref[:, c0:c0 + cw] = _dot(n, wr_ref[:, off + c0:off + c0 + cw]).astype(ref.dtype)
        off += width
    off = 0
    for ref, (_, height, _) in zip(col_refs, COL_GROUPS):
        for r0 in range(0, height, 512):
            rh = min(512, height - r0)
            ref[r0:r0 + rh, :] = _dot_nt(wc_ref[off + r0:off + r0 + rh, :], n).astype(ref.dtype)
        off += height


def _in_proj(x2, g_mix, w_row, w_col):
    t = x2.shape[0]
    row = lambda w: pl.BlockSpec((ROW_TILE, w), lambda i: (i, 0))
    col = lambda h: pl.BlockSpec((h, ROW_TILE), lambda i: (0, i))
    return pl.pallas_call(
        _in_proj_kernel,
        grid=(t // ROW_TILE,),
        in_specs=[row(D_MODEL), _resident((1, D_MODEL)), _resident((D_MODEL, ROW_WIDTH)),
                  _resident((COL_HEIGHT, D_MODEL))],
        out_specs=[row(w) for _, w, _ in ROW_GROUPS] + [col(h) for _, h, _ in COL_GROUPS],
        out_shape=[jax.ShapeDtypeStruct((t, w), dt) for _, w, dt in ROW_GROUPS]
        + [jax.ShapeDtypeStruct((h, t), dt) for _, h, dt in COL_GROUPS],
        compiler_params=_params("arbitrary"),
        name="in_proj",
    )(x2, g_mix, w_row, w_col)


def _compress_kernel(xk_ref, xv_ref, pos_ref, w1_ref, w2k_ref, w2vt_ref, kc_ref, vct_ref, *, n_chunk):
    lane = lax.broadcasted_iota(jnp.int32, (n_chunk, LANES), 1)
    half = CMP_LEN // 2
    for kv, x_ref in enumerate((xk_ref, xv_ref)):
        for h in range(LANES // HEAD_DIM):
            in_head = (lane >= h * HEAD_DIM) & (lane < (h + 1) * HEAD_DIM)
            top = jnp.zeros((n_chunk, CMP_HIDDEN), F32)
            bot = jnp.zeros((n_chunk, CMP_HIDDEN), F32)
            for l in range(half):
                x = x_ref[pl.ds(l, n_chunk, stride=CMP_STRIDE), :]
                xt = jnp.where(in_head, x + pos_ref[kv, l:l + 1, :], 0.0).astype(BF16)
                xb = jnp.where(in_head, x + pos_ref[kv, half + l:half + l + 1, :], 0.0).astype(BF16)
                top = top + _dot(xt, w1_ref[kv, l])
                bot = bot + _dot(xb, w1_ref[kv, half + l])
            pre = top + pltpu.roll(bot, n_chunk - 1, 0)
            hid = jax.nn.gelu(pre).astype(BF16)
            if kv == 0:
                kc_ref[h] = _dot(hid, w2k_ref[...]).astype(BF16)
            else:
                vct_ref[h] = _dot_nt(w2vt_ref[...], hid).astype(BF16)


def _compress(kvc, pos2, w1t, w2k, w2vt, batch, seq):
    n_chunk = seq // CMP_STRIDE
    pair = LANES // HEAD_DIM
    n_pair = N_KV_HEADS // pair
    kern = functools.partial(_compress_kernel, n_chunk=n_chunk)
    return pl.pallas_call(
        kern,
        grid=(batch, n_pair),
        in_specs=[
            pl.BlockSpec((seq, LANES), lambda b, hp: (b, hp)),
            pl.BlockSpec((seq, LANES), lambda b, hp: (b, n_pair + hp)),
            _resident((2, CMP_LEN, LANES)),
            _resident((2, CMP_LEN, LANES, CMP_HIDDEN)),
            _resident((CMP_HIDDEN, HEAD_DIM)),
            _resident((HEAD_DIM, CMP_HIDDEN)),
        ],
        out_specs=[
            pl.BlockSpec((None, pair, n_chunk, HEAD_DIM), lambda b, hp: (b, hp, 0, 0)),
            pl.BlockSpec((None, pair, HEAD_DIM, n_chunk), lambda b, hp: (b, hp, 0, 0)),
        ],
        out_shape=[
            jax.ShapeDtypeStruct((batch, N_KV_HEADS, n_chunk, HEAD_DIM), BF16),
            jax.ShapeDtypeStruct((batch, N_KV_HEADS, HEAD_DIM, n_chunk), BF16),
        ],
        compiler_params=_params("arbitrary", "arbitrary"),
        name="compress",
    )(kvc, kvc, pos2, w1t, w2k, w2vt)


def _nsa_attn_kernel(qt_ref, kc_ref, vct_ref, ovt_ref, ksw_ref, vswt_ref, gbrt_ref, o_ref,
                     ka_ref, kw_ref, vat_ref, vwt_ref, qa_ref, m_ref, acc_ref, rank_ref,
                     s0_ref, s1_ref, ocmp_ref, owin_ref,
                     *, seq, n_sel, n_top):
    i = pl.program_id(2)
    q0 = i * Q_TILE
    lanes = GROUP * Q_TILE
    n_chunk = seq // CMP_STRIDE
    n_blk = ovt_ref.shape[0]

    @pl.when(i == 0)
    def _stage_kv():
        blk_of_key = lax.broadcasted_iota(jnp.int32, (seq, HEAD_DIM), 0) // SEL_LEN
        lane = lax.broadcasted_iota(jnp.int32, (seq, HEAD_DIM), 1)
        onehot = jnp.where(blk_of_key == lane, 1.0, 0.0).astype(BF16)
        ksw = ksw_ref[...]
        ka_ref[...] = jnp.concatenate([ksw[:, :HEAD_DIM], onehot], axis=1)
        kw_ref[...] = jnp.concatenate([ksw[:, HEAD_DIM:], jnp.zeros((seq, HEAD_DIM), BF16)], axis=1)
        ones = jnp.ones((ONES_ROWS, K_TILE), BF16)
        for kt in range(seq // K_TILE):
            cs = slice(kt * K_TILE, (kt + 1) * K_TILE)
            vat_ref[kt] = jnp.concatenate([vswt_ref[:HEAD_DIM, cs], ones], axis=0)
            vwt_ref[kt] = jnp.concatenate([vswt_ref[HEAD_DIM:, cs], ones], axis=0)

    for g in range(GROUP):
        qa_ref[:HEAD_DIM, g * Q_TILE:(g + 1) * Q_TILE] = qt_ref[g * HEAD_DIM:(g + 1) * HEAD_DIM, :]

    c_tok = lax.broadcasted_iota(jnp.int32, (1, lanes), 1) % Q_TILE
    t_row = q0 + lax.broadcasted_iota(jnp.int32, (1, Q_TILE), 1)
    r_key = lax.broadcasted_iota(jnp.int32, (K_TILE, 1), 0)
    causal = r_key <= c_tok
    last_cmp = jnp.minimum((q0 + c_tok - (CMP_LEN - 1)) // CMP_STRIDE, n_chunk - 2)

    def normalized(acc):
        return acc[:HEAD_DIM] / jnp.maximum(acc[HEAD_DIM:HEAD_DIM + 1], 1e-30)

    def window_scores(early):
        def scores_w(kt):
            k0 = pl.multiple_of(kt * K_TILE, K_TILE)
            return _dot(kw_ref[pl.ds(k0, K_TILE), :HEAD_DIM], qa_ref[:HEAD_DIM, :])

        s_tail = scores_w(jnp.maximum(i - 2, 0))
        s_mid = scores_w(jnp.maximum(i - 1, 0))
        if early:
            s_tail = jnp.where(i >= 2, s_tail, NEG)
            s_mid = jnp.where(i >= 1, s_mid, NEG)
        return jnp.where(causal, scores_w(i), s_tail), s_mid

    def window_output(s_edge, s_mid):
        m = jnp.maximum(jnp.max(s_edge, axis=0, keepdims=True), jnp.max(s_mid, axis=0, keepdims=True))
        p_edge = jnp.exp2(s_edge - m)
        p_mid = jnp.exp2(s_mid - m).astype(BF16)
        owin_ref[...] = normalized(
            _dot(vwt_ref[i], jnp.where(causal, p_edge, 0.0).astype(BF16))
            + _dot(vwt_ref[jnp.maximum(i - 2, 0)], jnp.where(causal, 0.0, p_edge).astype(BF16))
            + _dot(vwt_ref[jnp.maximum(i - 1, 0)], p_mid))

    def cmp_and_select(nc, nb, early):
        n_col = lax.broadcasted_iota(jnp.int32, (nc, 1), 0)
        cmp_ok = n_col <= last_cmp
        s = jnp.where(cmp_ok, _dot(kc_ref[:nc, :], qa_ref[:HEAD_DIM, :]), -jnp.inf)
        s_edge, s_mid = window_scores(early)
        m = jnp.max(s, axis=0, keepdims=True)
        m = jnp.where(jnp.isfinite(m), m, 0.0)
        e = jnp.exp2(s - m)
        p = e / jnp.maximum(jnp.sum(e, axis=0, keepdims=True), 1e-30)
        ocmp_ref[...] = _dot(vct_ref[:, :nc], p.astype(BF16))
        p_sum = p[:, :Q_TILE]
        for g in range(1, GROUP):
            p_sum = p_sum + p[:, g * Q_TILE:(g + 1) * Q_TILE]

        ovt = ovt_ref[:nb, :nc]
        p1 = p_sum.astype(BF16)
        r1 = p_sum - p1.astype(F32)
        p2 = r1.astype(BF16)
        p3 = (r1 - p2.astype(F32)).astype(BF16)
        imp = _dot(ovt, p1) + _dot(ovt, p2) + _dot(ovt, p3)
        j_col = lax.broadcasted_iota(jnp.int32, (nb, 1), 0)
        cur = t_row // SEL_LEN
        forced = (j_col == 0) | (j_col == cur) | (j_col == cur - 1)
        valid = (j_col * SEL_LEN <= t_row) & (j_col < n_sel)
        v = jnp.where(valid, jnp.where(forced, RANK_FORCED, imp), RANK_INVALID)
        rank_ref[:nb, :] = v
        window_output(s_edge, s_mid)
        rank = jnp.zeros((nb, Q_TILE), jnp.int32)
        for jp in range(min(nb, n_sel)):
            row = rank_ref[jp:jp + 1, :]
            rank = rank + jnp.where(j_col > jp, jnp.where(row >= v, 1, 0), jnp.where(row > v, 1, 0))
        bias = jnp.where((rank < n_top) & valid, 0.0, NEG).astype(BF16)
        if nb < SEL_LEN:
            bias = jnp.concatenate([bias, jnp.full((SEL_LEN - nb, Q_TILE), NEG, BF16)], axis=0)
        for g in range(GROUP):
            qa_ref[HEAD_DIM:, g * Q_TILE:(g + 1) * Q_TILE] = bias

    n_class = -(-(seq // Q_TILE) // CLASS_TILES)
    for c in range(n_class):
        nb = min((c + 1) * CLASS_TILES * Q_TILE // SEL_LEN, n_blk)
        nc = min(-(-(c + 1) * CLASS_TILES * Q_TILE // CMP_STRIDE // LANES) * LANES, n_chunk)
        pl.when(i // CLASS_TILES == c)(functools.partial(cmp_and_select, nc, nb, c == 0))

    def scores(kt):
        k0 = pl.multiple_of(kt * K_TILE, K_TILE)
        return _dot(ka_ref[pl.ds(k0, K_TILE), :], qa_ref[...])

    def update(s, kt):
        m_old = m_ref[...]
        m_new = jnp.maximum(m_old, jnp.max(s, axis=0, keepdims=True))
        p = jnp.exp2(s - m_new).astype(BF16)
        acc_ref[...] = jnp.exp2(m_old - m_new) * acc_ref[...] + _dot(vat_ref[kt], p)
        m_ref[...] = m_new

    m_ref[...] = jnp.full((1, lanes), -jnp.inf, F32)
    acc_ref[...] = jnp.zeros((ACC_ROWS, lanes), F32)
    s0_ref[...] = jnp.where(causal, scores(i), NEG)
    s1_ref[...] = scores(0)
    update(s0_ref[...], i)

    def pair(j, carry):
        s0_ref[...] = scores(2 * j + 1)
        update(s1_ref[...], 2 * j)
        s1_ref[...] = scores(jnp.minimum(2 * j + 2, i - 1))
        update(s0_ref[...], 2 * j + 1)
        return carry

    lax.fori_loop(0, i // 2, pair, 0)

    def write_output():
        o_sel = normalized(acc_ref[...])
        o_cmp = ocmp_ref[...]
        o_win = owin_ref[...]
        gate = jax.nn.sigmoid(gbrt_ref[...])
        heads = []
        for g in range(GROUP):
            cs = slice(g * Q_TILE, (g + 1) * Q_TILE)
            heads.append(gate[g:g + 1] * o_cmp[:, cs]
                         + gate[GROUP + g:GROUP + g + 1] * o_sel[:, cs]
                         + gate[2 * GROUP + g:2 * GROUP + g + 1] * o_win[:, cs])
        o_ref[...] = jnp.concatenate(heads, axis=0).T.astype(o_ref.dtype)

    @pl.when(i % 2 == 1)
    def _():
        update(s1_ref[...], i - 1)
        write_output()

    @pl.when(i % 2 == 0)
    def _():
        write_output()


def _nsa_attn(qt, kc, vct, ovt, ksw, vswt, gbrt, batch, seq, n_sel, n_top):
    nq = seq // Q_TILE
    nkt = seq // K_TILE
    lanes = GROUP * Q_TILE
    n_chunk = seq // CMP_STRIDE
    n_blk = ovt.shape[0]
    kern = functools.partial(_nsa_attn_kernel, seq=seq, n_sel=n_sel, n_top=n_top)
    return pl.pallas_call(
        kern,
        grid=(batch, N_KV_HEADS, nq),
        in_specs=[
            pl.BlockSpec((GQ_COLS, Q_TILE), lambda b, h, i: (h, b * nq + i)),
            pl.BlockSpec((None, None, n_chunk, HEAD_DIM), lambda b, h, i: (b, h, 0, 0)),
            pl.BlockSpec((None, None, HEAD_DIM, n_chunk), lambda b, h, i: (b, h, 0, 0)),
            pl.BlockSpec((n_blk, n_chunk), lambda b, h, i: (0, 0)),
            pl.BlockSpec((seq, 2 * HEAD_DIM), lambda b, h, i: (b, h)),
            pl.BlockSpec((2 * HEAD_DIM, seq), lambda b, h, i: (h, b)),
            pl.BlockSpec((GATE_ROWS, Q_TILE), lambda b, h, i: (h, b * nq + i)),
        ],
        out_specs=pl.BlockSpec((Q_TILE, GQ_COLS), lambda b, h, i: (b * nq + i, h)),
        out_shape=jax.ShapeDtypeStruct((batch * seq, Q_COLS), BF16),
        scratch_shapes=[
            pltpu.VMEM((seq, LANES), BF16),
            pltpu.VMEM((seq, LANES), BF16),
            pltpu.VMEM((nkt, ACC_ROWS, K_TILE), BF16),
            pltpu.VMEM((nkt, ACC_ROWS, K_TILE), BF16),
            pltpu.VMEM((LANES, lanes), BF16),
            pltpu.VMEM((1, lanes), F32),
            pltpu.VMEM((ACC_ROWS, lanes), F32),
            pltpu.VMEM((n_blk, Q_TILE), F32),
            pltpu.VMEM((K_TILE, lanes), F32),
            pltpu.VMEM((K_TILE, lanes), F32),
            pltpu.VMEM((HEAD_DIM, lanes), F32),
            pltpu.VMEM((HEAD_DIM, lanes), F32),
        ],
        compiler_params=_params("arbitrary", "arbitrary", "arbitrary"),
        name="nsa_attn",
    )(qt, kc, vct, ovt, ksw, vswt, gbrt)


def _merge_kernel(o_ref, bch_ref, halo_ref, gmix_ref, x_ref, cw_ref, wa_ref, wc_ref, wo_ref, h_ref, *, seq):
    i = pl.program_id(0)
    bch = bch_ref[...].astype(F32)
    b_gate, u = bch[:, :CONV_DIM], bch[:, CONV_DIM:2 * CONV_DIM] * bch[:, 2 * CONV_DIM:]
    halo = halo_ref[...].astype(F32)
    u_halo = halo[:, CONV_DIM:2 * CONV_DIM] * halo[:, 2 * CONV_DIM:]
    u_halo = jnp.where((i * ROW_TILE) % seq == 0, 0.0, u_halo)
    row = lax.broadcasted_iota(jnp.int32, (ROW_TILE, 1), 0)
    u1 = jnp.where(row == 0, u_halo[7:8], pltpu.roll(u, 1, 0))
    u2 = jnp.where(row == 0, u_halo[6:7], jnp.where(row == 1, u_halo[7:8], pltpu.roll(u, 2, 0)))
    cw = cw_ref[...]
    conv = u2 * cw[0:1] + u1 * cw[1:2] + u * cw[2:3]
    y_conv = _dot((b_gate * conv).astype(BF16), wc_ref[...])
    y_attn = _dot(o_ref[...], wa_ref[...])
    gmix = gmix_ref[...].astype(F32)
    mix = jax.nn.sigmoid(gmix[:, :D_MODEL]) * y_conv + jax.nn.sigmoid(gmix[:, D_MODEL:]) * y_attn
    h_ref[...] = x_ref[...] + _dot(mix.astype(BF16), wo_ref[...])


def _merge(o, bch, gmix, x2, conv_w, w_attn_out, w_conv_out, w_o, seq):
    t = x2.shape[0]
    row = lambda w: pl.BlockSpec((ROW_TILE, w), lambda i: (i, 0))
    halo_blocks = ROW_TILE // 8
    kern = functools.partial(_merge_kernel, seq=seq)
    return pl.pallas_call(
        kern,
        grid=(t // ROW_TILE,),
        in_specs=[
            row(Q_COLS), row(3 * CONV_DIM),
            pl.BlockSpec((8, 3 * CONV_DIM), lambda i: (jnp.maximum(i * halo_blocks - 1, 0), 0)),
            row(2 * D_MODEL), row(D_MODEL),
            _resident((CONV_WIDTH, CONV_DIM)),
            _resident((Q_COLS, D_MODEL)), _resident((CONV_DIM, D_MODEL)), _resident((D_MODEL, D_MODEL)),
        ],
        out_specs=row(D_MODEL),
        out_shape=jax.ShapeDtypeStruct((t, D_MODEL), F32),
        compiler_params=_params("arbitrary"),
        name="merge",
    )(o, bch, bch, gmix, x2, conv_w, w_attn_out, w_conv_out, w_o)


FF_CHUNK = 256


def _ffn_kernel(h_ref, gf_ref, gl_ref, wg_ref, wu_ref, wd_ref, out_ref, act_ref):
    h = h_ref[...]
    n = _rmsnorm(h, gf_ref[...]).astype(BF16)
    for c0 in range(0, D_FF, FF_CHUNK):
        cs = slice(c0, c0 + FF_CHUNK)
        act_ref[:, cs] = (jax.nn.silu(_dot(n, wg_ref[:, cs])) * _dot(n, wu_ref[:, cs])).astype(BF16)
    h = h + _dot(act_ref[...], wd_ref[...])
    out_ref[...] = _rmsnorm(h, gl_ref[...])


def _ffn(h1, g_ffn, g_final, w_gate, w_up, w_down):
    t = h1.shape[0]
    row = pl.BlockSpec((ROW_TILE, D_MODEL), lambda i: (i, 0))
    return pl.pallas_call(
        _ffn_kernel,
        grid=(t // ROW_TILE,),
        in_specs=[row, _resident((1, D_MODEL)), _resident((1, D_MODEL)),
                  _resident((D_MODEL, D_FF)), _resident((D_MODEL, D_FF)), _resident((D_FF, D_MODEL))],
        out_specs=row,
        out_shape=jax.ShapeDtypeStruct((t, D_MODEL), F32),
        scratch_shapes=[pltpu.VMEM((ROW_TILE, D_FF), BF16)],
        compiler_params=_params("arbitrary"),
        name="ffn",
    )(h1, g_ffn, g_final, w_gate, w_up, w_down)


def _overlap_t(n_chunk, n_sel, n_blk):
    cs = np.arange(n_chunk)[None, :] * CMP_STRIDE
    ss = np.arange(n_blk)[:, None] * SEL_LEN
    ov = np.maximum(0, np.minimum(cs + CMP_LEN, ss + SEL_LEN) - np.maximum(cs, ss)) / CMP_LEN
    ov[n_sel:, :] = 0
    ov[:, n_chunk - 1] = 0
    return jnp.asarray(ov, dtype=BF16)


def _tile_heads(w1):
    pair = LANES // HEAD_DIM
    w1 = w1.reshape(CMP_LEN, 1, HEAD_DIM, CMP_HIDDEN)
    return jnp.broadcast_to(w1, (CMP_LEN, pair, HEAD_DIM, CMP_HIDDEN)).reshape(CMP_LEN, LANES, CMP_HIDDEN)


def kernel(x, w_in, conv_w, w_conv_out, cmp_pos_k, cmp_w1_k, cmp_w2_k, cmp_pos_v, cmp_w1_v, cmp_w2_v,
           w_attn_out, w_o, g_mix, g_ffn, w_gate, w_up, w_down, g_final):
    batch, seq, _ = x.shape
    assert w_in.shape[0] == 1, "one layer"
    assert seq % Q_TILE == 0 and (batch * seq) % ROW_TILE == 0 and seq % ROW_TILE == 0
    n_sel = seq // SEL_LEN
    assert n_sel <= SEL_LEN, "the block one-hot occupies SEL_LEN lanes"
    n_top = min(SEL_TOP, n_sel)
    n_chunk = seq // CMP_STRIDE
    n_blk = -(-n_sel // 16) * 16

    x2 = x.reshape(batch * seq, D_MODEL)
    w_row, w_col = _arrange_w_in(w_in[0])
    bch, kvc, ksw, gmix, qt, vswt, gbrt = _in_proj(x2, g_mix[0][None], w_row, w_col)

    pair = LANES // HEAD_DIM
    pos2 = jnp.stack([jnp.tile(cmp_pos_k[0], (1, pair)), jnp.tile(cmp_pos_v[0], (1, pair))])
    w1t = jnp.stack([_tile_heads(cmp_w1_k[0]), _tile_heads(cmp_w1_v[0])]).astype(BF16)
    kc, vct = _compress(kvc, pos2, w1t, cmp_w2_k[0].astype(BF16), cmp_w2_v[0].T.astype(BF16), batch, seq)

    o = _nsa_attn(qt, kc, vct, _overlap_t(n_chunk, n_sel, n_blk), ksw, vswt, gbrt, batch, seq, n_sel, n_top)

    h1 = _merge(o, bch, gmix, x2, conv_w[0], w_attn_out[0].astype(BF16), w_conv_out[0].astype(BF16),
                w_o[0].astype(BF16), seq)
    out = _ffn(h1, g_ffn[0][None], g_final[None], w_gate[0].astype(BF16), w_up[0].astype(BF16),
               w_down[0].astype(BF16))
    return out.reshape(batch, seq, D_MODEL)
```
